```python
import jax, jax.numpy as jnp
from jax import lax
import numpy as np

D_MODEL = 1024
BATCH = 16
SEQ = 2048
DEPTH = 2

GRID_W = 64
CTX_LEN = 256

N_EVEN = (DEPTH + 1) // 2
N_ODD = DEPTH // 2

N_HEADS = 8
Q_LORA = 384
KV_LORA = 256
NOPE_DIM = 64
ROPE_DIM = 32
QK_DIM = NOPE_DIM + ROPE_DIM
V_DIM = 64
ROPE_THETA = 10000.0
Q_BLOCK = 128

CONV_CH = 512
CONV_W = 31

A_IN = Q_LORA + KV_LORA + ROPE_DIM + 2 * CONV_CH
MIX_WIDTH = N_HEADS * V_DIM + CONV_CH

SG_DIM = 2 * D_MODEL
CHUNK = 128
SG_GROUPS = 8
SG_CH = SG_DIM // SG_GROUPS

N_GROUPS = 4
EXPERTS_PER_GROUP = 8
N_EXPERTS = N_GROUPS * EXPERTS_PER_GROUP
TOP_K = 2
D_EXPERT = 256

kernel_name = "hybrid_mla_conformer_gmlp_hmoe_dit"


def rms_norm(x, g, eps=1e-6):
    xf = x.astype(jnp.float32)
    y = xf * lax.rsqrt(jnp.mean(xf * xf, axis=-1, keepdims=True) + eps)
    return (y * g.astype(jnp.float32)).astype(x.dtype)


def layer_norm(x, g, b, eps=1e-5):
    xf = x.astype(jnp.float32)
    mu = jnp.mean(xf, axis=-1, keepdims=True)
    var = jnp.mean(jnp.square(xf - mu), axis=-1, keepdims=True)
    y = (xf - mu) * lax.rsqrt(var + eps)
    return (y * g.astype(jnp.float32) + b.astype(jnp.float32)).astype(x.dtype)


def modulate(h, shift, scale):
    return h * (1 + scale) + shift


def axial_rope(n_tokens, dtype):
    rows = n_tokens // GRID_W
    t = jnp.arange(rows * GRID_W)
    row = (t // GRID_W).astype(jnp.float32)
    col = (t % GRID_W).astype(jnp.float32)
    n_freq = ROPE_DIM // 4
    inv_freq = ROPE_THETA ** (-jnp.arange(n_freq, dtype=jnp.float32) / n_freq)
    ang = jnp.concatenate([row[:, None] * inv_freq, col[:, None] * inv_freq], axis=-1)
    return jnp.cos(ang).astype(dtype), jnp.sin(ang).astype(dtype)


def rope_tail(x, cos, sin):
    half = ROPE_DIM // 2
    xn = x[..., :NOPE_DIM]
    x1 = x[..., NOPE_DIM:NOPE_DIM + half]
    x2 = x[..., NOPE_DIM + half:]
    cs = cos[None, :, None, :]
    sn = sin[None, :, None, :]
    return jnp.concatenate([xn, x1 * cs - x2 * sn, x1 * sn + x2 * cs], axis=-1)


def mla_queries(q_lat, q_norm_g, w_uq, q_g, rope):
    q = rms_norm(q_lat, q_norm_g) @ w_uq
    q = q.reshape(q.shape[:-1] + (N_HEADS, QK_DIM))
    q = rms_norm(q, q_g)
    if rope is not None:
        q = rope_tail(q, *rope)
    return q


def mla_keys_values(kv_lat, k_rope, kv_norm_g, w_ukv, k_g, rope):
    kv = rms_norm(kv_lat, kv_norm_g) @ w_ukv
    kv = kv.reshape(kv.shape[:-1] + (N_HEADS, NOPE_DIM + V_DIM))
    k_nope, v = kv[..., :NOPE_DIM], kv[..., NOPE_DIM:]
    kr = jnp.broadcast_to(k_rope[..., None, :], k_nope.shape[:-1] + (ROPE_DIM,))
    k = rms_norm(jnp.concatenate([k_nope, kr], axis=-1), k_g)
    if rope is not None:
        k = rope_tail(k, *rope)
    return k, v


def attend(q, k, v):
    s = jnp.einsum('bqhd,bkhd->bhqk', q, k).astype(jnp.float32) * (QK_DIM ** -0.5)
    p = jax.nn.softmax(s, axis=-1).astype(v.dtype)
    return jnp.einsum('bhqk,bkhd->bqhd', p, v)


def latent_attention(q, k_all, v_all):
    B, S = q.shape[0], q.shape[1]
    nb = S // Q_BLOCK
    qb = q.reshape(B, nb, Q_BLOCK, N_HEADS, QK_DIM).transpose(1, 0, 2, 3, 4)
    o = lax.map(lambda qblk: attend(qblk, k_all, v_all), qb)
    return o.transpose(1, 0, 2, 3, 4).reshape(B, S, N_HEADS * V_DIM)


def conformer_conv(g, w_dw, b_dw, ln_g, ln_b):
    a, b = jnp.split(g, 2, axis=-1)
    z = a * jax.nn.sigmoid(b)
    z = lax.conv_general_dilated(
        z, w_dw[:, None, :].astype(z.dtype), window_strides=(1,),
        padding=[(CONV_W // 2, CONV_W // 2)],
        dimension_numbers=('NWC', 'WIO', 'NWC'), feature_group_count=CONV_CH) + b_dw
    return jax.nn.silu(layer_norm(z, ln_g, ln_b))


def mla_conv_mixer(h, hc, w_in, q_norm_g, w_uq, kv_norm_g, w_ukv, q_g, k_g,
                   w_dw, b_dw, ln_g, ln_b, w_out, ctx_out):
    S = h.shape[1]
    splits = [Q_LORA, Q_LORA + KV_LORA, Q_LORA + KV_LORA + ROPE_DIM]
    q_lat, kv_lat, kr_lat, glu_lat = jnp.split(h @ w_in, splits, axis=-1)
    rope = axial_rope(S, h.dtype)
    q = mla_queries(q_lat, q_norm_g, w_uq, q_g, rope)
    k, v = mla_keys_values(kv_lat, kr_lat, kv_norm_g, w_ukv, k_g, rope)
    if ctx_out:
        q_c_lat, kv_c, kr_c, glu_c = jnp.split(hc @ w_in, splits, axis=-1)
    else:
        kv_c, kr_c = jnp.split(hc @ w_in[:, Q_LORA:Q_LORA + KV_LORA + ROPE_DIM], [KV_LORA], axis=-1)
    k_c, v_c = mla_keys_values(kv_c, kr_c, kv_norm_g, w_ukv, k_g, None)
    k_all = jnp.concatenate([k_c, k], axis=1)
    v_all = jnp.concatenate([v_c, v], axis=1)
    attn = latent_attention(q, k_all, v_all)
    conv = conformer_conv(glu_lat, w_dw, b_dw, ln_g, ln_b)
    y = jnp.concatenate([attn, conv], axis=-1) @ w_out
    if not ctx_out:
        return y, None
    q_c = mla_queries(q_c_lat, q_norm_g, w_uq, q_g, None)
    attn_c = attend(q_c, k_c, v_c).reshape(hc.shape[0], hc.shape[1], N_HEADS * V_DIM)
    conv_c = conformer_conv(glu_c, w_dw, b_dw, ln_g, ln_b)
    yc = jnp.concatenate([attn_c, conv_c], axis=-1) @ w_out
    return y, yc


def chunk_gating_mlp(h, w_in, b_in, ln_g, ln_b, w_s, b_s, w_out):
    B, N, _ = h.shape
    z = jax.nn.gelu(h @ w_in + b_in)
    u, v = jnp.split(z, 2, axis=-1)
    v = layer_norm(v, ln_g, ln_b).reshape(B, N // CHUNK, CHUNK, SG_GROUPS, SG_CH)
    s = jnp.einsum('gpq,bnqgc->bnpgc', w_s, v) + b_s.T[:, :, None]
    return (u * s.reshape(B, N, SG_DIM)) @ w_out


def hier_moe(h, w_group, b_group, w_router, b_router, w_gate, w_up, w_down):
    shp = h.shape
    t = h.reshape(-1, shp[-1])
    gl = (t @ w_group + b_group).astype(jnp.float32)
    gp = jax.nn.softmax(gl, axis=-1)
    gidx = jnp.argmax(gl, axis=-1)
    gw = jnp.take_along_axis(gp, gidx[:, None], axis=-1)
    el = (jnp.einsum('td,gde->tge', t, w_router) + b_router).astype(jnp.float32)
    el = jnp.take_along_axis(el, gidx[:, None, None], axis=1)[:, 0]
    tv, ti = lax.top_k(el, TOP_K)
    ew = jax.nn.softmax(tv, axis=-1) * gw
    eid = gidx[:, None] * EXPERTS_PER_GROUP + ti
    dense_w = jnp.sum(jax.nn.one_hot(eid, N_EXPERTS, dtype=jnp.float32) * ew[..., None], axis=1)
    dense_w = dense_w.astype(t.dtype)
    out = jnp.zeros_like(t)
    for e in range(N_EXPERTS):
        he = (jax.nn.silu(t @ w_gate[e]) * (t @ w_up[e])) @ w_down[e]
        out = out + dense_w[:, e:e + 1] * he
    return out.reshape(shp)


def setup_inputs(seed: int = 0) -> dict:
    key = jax.random.key(seed)
    ks = iter(jax.random.split(key, 40))

    def nrm(shape, scale):
        return jax.random.normal(next(ks), shape, jnp.float32) * scale

    def gain(shape):
        return 1.0 + nrm(shape, 0.02)

    D = D_MODEL
    NE, NO = N_EVEN, N_ODD
    return {
        "x": nrm((BATCH, SEQ, D), 1.0),
        "c": nrm((BATCH, D), 1.0),
        "ctx": nrm((BATCH, CTX_LEN, D), 1.0),
        "c_ctx": nrm((D,), 1.0),
        "w_ada": nrm((DEPTH, D, 6 * D), 0.5 * D ** -0.5),
        "b_ada": nrm((DEPTH, 6 * D), 0.02),
        "norm1_g": gain((DEPTH, D)),
        "norm2_g": gain((DEPTH, D)),
        "a_w_in": nrm((NE, D, A_IN), D ** -0.5),
        "a_q_norm_g": gain((NE, Q_LORA)),
        "a_w_uq": nrm((NE, Q_LORA, N_HEADS * QK_DIM), Q_LORA ** -0.5),
        "a_kv_norm_g": gain((NE, KV_LORA)),
        "a_w_ukv": nrm((NE, KV_LORA, N_HEADS * (NOPE_DIM + V_DIM)), KV_LORA ** -0.5),
        "a_q_g": gain((NE, QK_DIM)),
        "a_k_g": gain((NE, QK_DIM)),
        "b_w_dw": nrm((NE, CONV_W, CONV_CH), CONV_W ** -0.5),
        "b_b_dw": nrm((NE, CONV_CH), 0.02),
        "b_ln_g": gain((NE, CONV_CH)),
        "b_ln_b": nrm((NE, CONV_CH), 0.02),
        "ab_w_out": nrm((NE, MIX_WIDTH, D), MIX_WIDTH ** -0.5),
        "c_w_in": nrm((NO, D, 2 * SG_DIM), D ** -0.5),
        "c_b_in": nrm((NO, 2 * SG_DIM), 0.02),
        "c_ln_g": gain((NO, SG_DIM)),
        "c_ln_b": nrm((NO, SG_DIM), 0.02),
        "c_w_s": nrm((NO, SG_GROUPS, CHUNK, CHUNK), CHUNK ** -0.5),
        "c_b_s": gain((NO, SG_GROUPS, CHUNK)),
        "c_w_out": nrm((NO, SG_DIM, D), SG_DIM ** -0.5),
        "moe_w_group": nrm((DEPTH, D, N_GROUPS), D ** -0.5),
        "moe_b_group": nrm((DEPTH, N_GROUPS), 0.01),
        "moe_w_router": nrm((DEPTH, N_GROUPS, D, EXPERTS_PER_GROUP), D ** -0.5),
        "moe_b_router": nrm((DEPTH, N_GROUPS, EXPERTS_PER_GROUP), 0.01),
        "moe_w_gate": nrm((DEPTH, N_EXPERTS, D, D_EXPERT), D ** -0.5),
        "moe_w_up": nrm((DEPTH, N_EXPERTS, D, D_EXPERT), D ** -0.5),
        "moe_w_down": nrm((DEPTH, N_EXPERTS, D_EXPERT, D), D_EXPERT ** -0.5),
    }


def reference(x, c, ctx, c_ctx, w_ada, b_ada, norm1_g, norm2_g,
              a_w_in, a_q_norm_g, a_w_uq, a_kv_norm_g, a_w_ukv, a_q_g, a_k_g,
              b_w_dw, b_b_dw, b_ln_g, b_ln_b, ab_w_out,
              c_w_in, c_b_in, c_ln_g, c_ln_b, c_w_s, c_b_s, c_w_out,
              moe_w_group, moe_b_group, moe_w_router, moe_b_router,
              moe_w_gate, moe_w_up, moe_w_down):
    xc = ctx
    silu_c = jax.nn.silu(c)
    silu_cc = jax.nn.silu(c_ctx)
    for l in range(DEPTH):
        i = l // 2
        ctx_later = any(j % 2 == 0 for j in range(l + 1, DEPTH))
        ctx_here = (l % 2 == 0) or ctx_later
        mod = (silu_c @ w_ada[l] + b_ada[l])[:, None, :]
        sh1, sc1, g1, sh2, sc2, g2 = jnp.split(mod, 6, axis=-1)
        h = modulate(rms_norm(x, norm1_g[l]), sh1, sc1)
        hc = None
        if ctx_here:
            mod_c = silu_cc @ w_ada[l] + b_ada[l]
            csh1, csc1, cg1, csh2, csc2, cg2 = jnp.split(mod_c, 6, axis=-1)
            hc = modulate(rms_norm(xc, norm1_g[l]), csh1, csc1)
        if l % 2 == 0:
            y, yc = mla_conv_mixer(h, hc, a_w_in[i], a_q_norm_g[i], a_w_uq[i], a_kv_norm_g[i],
                                   a_w_ukv[i], a_q_g[i], a_k_g[i], b_w_dw[i], b_b_dw[i],
                                   b_ln_g[i], b_ln_b[i], ab_w_out[i], ctx_later)
        else:
            y = chunk_gating_mlp(h, c_w_in[i], c_b_in[i], c_ln_g[i], c_ln_b[i],
                                 c_w_s[i], c_b_s[i], c_w_out[i])
            yc = None
            if ctx_later:
                yc = chunk_gating_mlp(hc, c_w_in[i], c_b_in[i], c_ln_g[i], c_ln_b[i],
                                      c_w_s[i], c_b_s[i], c_w_out[i])
        x = x + g1 * y
        moe_args = (moe_w_group[l], moe_b_group[l], moe_w_router[l], moe_b_router[l],
                    moe_w_gate[l], moe_w_up[l], moe_w_down[l])
        h2 = modulate(rms_norm(x, norm2_g[l]), sh2, sc2)
        x = x + g2 * hier_moe(h2, *moe_args)
        if ctx_later:
            xc = xc + cg1 * yc
            hc2 = modulate(rms_norm(xc, norm2_g[l]), csh2, csc2)
            xc = xc + cg2 * hier_moe(hc2, *moe_args)
    return x
```

```python
import functools
import math

import jax
import jax.numpy as jnp
from jax import lax
from jax.experimental import pallas as pl
from jax.experimental.pallas import tpu as pltpu

F32 = jnp.float32
BF16 = jnp.bfloat16
HIGHEST = lax.Precision.HIGHEST

N_HEADS = 8
Q_LORA = 384
KV_LORA = 256
NOPE_DIM = 64
ROPE_DIM = 32
QK_DIM = NOPE_DIM + ROPE_DIM
V_DIM = 64
ROPE_THETA = 10000.0
GRID_W = 64
CONV_CH = 512
CONV_W = 31
CHUNK = 128
SG_GROUPS = 8
N_GROUPS = 4
EXPERTS_PER_GROUP = 8
N_EXPERTS = N_GROUPS * EXPERTS_PER_GROUP
D_EXPERT = 256

LANES = 128
HEAD_PAD = LANES
HALO = 16

TOK_TILE = 256
Q_TILE = 256
ROW_TILE = 256
VMEM_LIMIT = 56 << 20


def _params(sem, vmem=VMEM_LIMIT):
    return pltpu.CompilerParams(dimension_semantics=sem, vmem_limit_bytes=vmem)


def _const_spec(shape):
    nd = len(shape)
    return pl.BlockSpec(shape, lambda *_: (0,) * nd)


def _rms(x, eps=1e-6):
    return x * lax.rsqrt(jnp.mean(x * x, axis=-1, keepdims=True) + eps)


def _silu(x):
    return x * jax.nn.sigmoid(x)


def _ada_kernel(c_ref, w_ref, b_ref, o_ref):
    sc = _silu(c_ref[...])
    o_ref[0] = jnp.dot(sc, w_ref[0], preferred_element_type=F32, precision=HIGHEST) + b_ref[0]


def _ada(cvec, w_ada, b_ada):
    depth, d, n = w_ada.shape
    rows = cvec.shape[0]
    tn = 1536
    return pl.pallas_call(
        _ada_kernel,
        grid=(depth, n // tn),
        in_specs=[
            pl.BlockSpec((rows, d), lambda l, j: (0, 0)),
            pl.BlockSpec((1, d, tn), lambda l, j: (l, 0, j)),
            pl.BlockSpec((1, 1, tn), lambda l, j: (l, 0, j)),
        ],
        out_specs=pl.BlockSpec((1, rows, tn), lambda l, j: (l, 0, j)),
        out_shape=jax.ShapeDtypeStruct((depth, rows, n), F32),
        compiler_params=_params(("arbitrary", "arbitrary")),
    )(cvec, w_ada, b_ada.reshape(depth, 1, n))


def _rope(t, rc, rs1, rs2):
    return t * rc + pltpu.roll(t, HEAD_PAD - ROPE_DIM // 2, 1) * rs1 + pltpu.roll(t, ROPE_DIM // 2, 1) * rs2


def _head_norm(t, gain):
    ss = jnp.sum(t * t, axis=-1, keepdims=True) * (1.0 / QK_DIM)
    return t * lax.rsqrt(ss + 1e-6) * gain


def _kv_heads(kvl, kr, kvng, wuk_ref, wuv_ref, vones, kg, rope, k_ref, v_ref):
    kvn = (_rms(kvl) * kvng).astype(BF16)
    kfull = jnp.dot(kvn, wuk_ref[...], preferred_element_type=F32)
    vfull = jnp.dot(kvn, wuv_ref[...], preferred_element_type=F32) + vones
    for h in range(N_HEADS):
        sl = slice(h * HEAD_PAD, (h + 1) * HEAD_PAD)
        kn = _head_norm(kfull[:, sl] + kr, kg)
        if rope is not None:
            kn = _rope(kn, *rope)
        k_ref[0, h] = kn.astype(BF16)
        v_ref[0, h] = vfull[:, sl].astype(BF16)


def _modulated_norm(x, mod, gain, shift_row, scale_row):
    return _rms(x) * gain * (1.0 + mod[scale_row:scale_row + 1]) + mod[shift_row:shift_row + 1]


def _mla_proj_kernel(x_ref, mod_ref, n1g_ref, win_ref, qng_ref, wuq_ref, kvng_ref, wuk_ref, wuv_ref,
                     vones_ref, qg_ref, kg_ref, rc_ref, rs1_ref, rs2_ref,
                     q_ref, k_ref, v_ref, z_ref):
    h = _modulated_norm(x_ref[0], mod_ref[0], n1g_ref[...], 0, 1).astype(BF16)
    proj = jnp.dot(h, win_ref[...], preferred_element_type=F32)
    o_kv = Q_LORA
    o_a = o_kv + KV_LORA
    o_b = o_a + CONV_CH
    o_kr = o_b + CONV_CH
    rope = (rc_ref[...], rs1_ref[...], rs2_ref[...])
    qn = (_rms(proj[:, :o_kv]) * qng_ref[...]).astype(BF16)
    qfull = jnp.dot(qn, wuq_ref[...], preferred_element_type=F32)
    qg = qg_ref[...]
    for hd in range(N_HEADS):
        sl = slice(hd * HEAD_PAD, (hd + 1) * HEAD_PAD)
        q_ref[0, hd] = _rope(_head_norm(qfull[:, sl], qg), *rope).astype(BF16)
    _kv_heads(proj[:, o_kv:o_a], proj[:, o_kr:o_kr + HEAD_PAD], kvng_ref[...], wuk_ref, wuv_ref,
              vones_ref[...], kg_ref[...], rope, k_ref, v_ref)
    z_ref[0] = (proj[:, o_a:o_b] * jax.nn.sigmoid(proj[:, o_b:o_kr])).astype(BF16)


def _ctx_proj_kernel(x_ref, mod_ref, n1g_ref, win_ref, kvng_ref, wuk_ref, wuv_ref, vones_ref, kg_ref,
                     k_ref, v_ref):
    h = _modulated_norm(x_ref[0], mod_ref[0], n1g_ref[...], 0, 1).astype(BF16)
    proj = jnp.dot(h, win_ref[...], preferred_element_type=F32)
    _kv_heads(proj[:, :KV_LORA], proj[:, KV_LORA:KV_LORA + HEAD_PAD], kvng_ref[...], wuk_ref, wuv_ref,
              vones_ref[...], kg_ref[...], None, k_ref, v_ref)


def _rope_tables(s):
    t = jnp.arange(s)
    row = (t // GRID_W).astype(F32)
    col = (t % GRID_W).astype(F32)
    n_freq = ROPE_DIM // 4
    inv_freq = ROPE_THETA ** (-jnp.arange(n_freq, dtype=F32) / n_freq)
    ang = jnp.concatenate([row[:, None] * inv_freq, col[:, None] * inv_freq], axis=-1)
    cos, sin = jnp.cos(ang), jnp.sin(ang)
    half = ROPE_DIM // 2
    zeros = jnp.zeros((s, half), F32)
    tail = jnp.zeros((s, HEAD_PAD - QK_DIM), F32)
    rc = jnp.concatenate([jnp.ones((s, NOPE_DIM), F32), cos, cos, tail], axis=-1)
    rs1 = jnp.concatenate([jnp.zeros((s, NOPE_DIM), F32), -sin, zeros, tail], axis=-1)
    rs2 = jnp.concatenate([jnp.zeros((s, NOPE_DIM), F32), zeros, sin, tail], axis=-1)
    return rc, rs1, rs2


def _pad_heads(w, width, lane_off=None):
    k = w.shape[0]
    w = w.reshape(k, N_HEADS, width)
    out = jnp.zeros((k, N_HEADS, HEAD_PAD), w.dtype)
    if lane_off is None:
        out = out.at[:, :, :width].set(w)
    else:
        for h in range(N_HEADS):
            out = out.at[:, h, lane_off(h):lane_off(h) + width].set(w[:, h])
    return out.reshape(k, N_HEADS * HEAD_PAD)


def _v_lane(h):
    return (h % 2) * V_DIM


def _ones_lane(h):
    return V_DIM if h % 2 == 0 else 0


def _mla_weights(w_in, w_uq, w_ukv, q_g, k_g):
    d = w_in.shape[0]
    o1, o2, o3 = Q_LORA, Q_LORA + KV_LORA, Q_LORA + KV_LORA + ROPE_DIM
    kr_block = jnp.zeros((d, HEAD_PAD), w_in.dtype).at[:, NOPE_DIM:QK_DIM].set(w_in[:, o2:o3])
    glu = w_in[:, o3:]
    win_lat = jnp.concatenate([w_in[:, :o1], w_in[:, o1:o2], glu, kr_block], axis=1).astype(BF16)
    win_ctx = jnp.concatenate([w_in[:, o1:o2], kr_block], axis=1).astype(BF16)
    wuq = _pad_heads(w_uq, QK_DIM).astype(BF16)
    ukv = w_ukv.reshape(KV_LORA, N_HEADS, NOPE_DIM + V_DIM)
    wuk = _pad_heads(ukv[:, :, :NOPE_DIM].reshape(KV_LORA, -1), NOPE_DIM).astype(BF16)
    wuv = _pad_heads(ukv[:, :, NOPE_DIM:].reshape(KV_LORA, -1), V_DIM, _v_lane).astype(BF16)
    vones = jnp.zeros((N_HEADS, HEAD_PAD), F32)
    for h in range(N_HEADS):
        vones = vones.at[h, _ones_lane(h)].set(1.0)
    vones = vones.reshape(1, N_HEADS * HEAD_PAD)
    pad = jnp.zeros((HEAD_PAD - QK_DIM,), F32)
    qg = (jnp.concatenate([q_g, pad]) * (QK_DIM ** -0.5 * math.log2(math.e))).reshape(1, HEAD_PAD)
    kg = jnp.concatenate([k_g, pad]).reshape(1, HEAD_PAD)
    return win_lat, win_ctx, wuq, wuk, wuv, vones, qg, kg


def _mla_proj(x, mod, n1g, win, qng, wuq, kvng, wuk, wuv, vones, qg, kg, rope):
    b, s, d = x.shape
    tm = TOK_TILE
    hp = N_HEADS * HEAD_PAD
    head_out = jax.ShapeDtypeStruct((b, N_HEADS, s, HEAD_PAD), BF16)
    head_spec = pl.BlockSpec((1, N_HEADS, tm, HEAD_PAD), lambda i, j: (i, 0, j, 0))
    rope_spec = pl.BlockSpec((tm, HEAD_PAD), lambda i, j: (j, 0))
    return pl.pallas_call(
        _mla_proj_kernel,
        grid=(b, s // tm),
        in_specs=[
            pl.BlockSpec((1, tm, d), lambda i, j: (i, j, 0)),
            pl.BlockSpec((1, 6, d), lambda i, j: (i, 0, 0)),
            _const_spec((1, d)), _const_spec(win.shape), _const_spec((1, Q_LORA)), _const_spec(wuq.shape),
            _const_spec((1, KV_LORA)), _const_spec(wuk.shape), _const_spec(wuv.shape), _const_spec((1, hp)),
            _const_spec((1, HEAD_PAD)), _const_spec((1, HEAD_PAD)), rope_spec, rope_spec, rope_spec,
        ],
        out_specs=[head_spec, head_spec, head_spec, pl.BlockSpec((1, tm, CONV_CH), lambda i, j: (i, j, 0))],
        out_shape=[head_out, head_out, head_out, jax.ShapeDtypeStruct((b, s, CONV_CH), BF16)],
        compiler_params=_params(("arbitrary", "arbitrary")),
    )(x, mod, n1g, win, qng, wuq, kvng, wuk, wuv, vones, qg, kg, *rope)


def _ctx_proj(ctx, mod_c, n1g, win, kvng, wuk, wuv, vones, kg):
    b, s, d = ctx.shape
    tm = min(TOK_TILE, s)
    hp = N_HEADS * HEAD_PAD
    head_out = jax.ShapeDtypeStruct((b, N_HEADS, s, HEAD_PAD), BF16)
    head_spec = pl.BlockSpec((1, N_HEADS, tm, HEAD_PAD), lambda i, j: (i, 0, j, 0))
    return pl.pallas_call(
        _ctx_proj_kernel,
        grid=(b, s // tm),
        in_specs=[
            pl.BlockSpec((1, tm, d), lambda i, j: (i, j, 0)),
            _const_spec((1, 6, d)),
            _const_spec((1, d)), _const_spec(win.shape), _const_spec((1, KV_LORA)), _const_spec(wuk.shape),
            _const_spec(wuv.shape), _const_spec((1, hp)), _const_spec((1, HEAD_PAD)),
        ],
        out_specs=[head_spec, head_spec],
        out_shape=[head_out, head_out],
        compiler_params=_params(("arbitrary", "arbitrary")),
    )(ctx, mod_c, n1g, win, kvng, wuk, wuv, vones, kg)


def _attn_kernel(q_ref, kx_ref, kc_ref, vx_ref, vc_ref, o_ref):
    nt = (((1,), (1,)), ((), ()))
    outs = []
    for j in range(2):
        q = q_ref[0, j]
        sx = lax.dot_general(q, kx_ref[0, j], nt, preferred_element_type=F32)
        sc = lax.dot_general(q, kc_ref[0, j], nt, preferred_element_type=F32)
        m = jnp.maximum(jnp.max(sx, axis=-1, keepdims=True), jnp.max(sc, axis=-1, keepdims=True))
        px = jnp.exp2(sx - m).astype(BF16)
        pc = jnp.exp2(sc - m).astype(BF16)
        o = (jnp.dot(px, vx_ref[0, j], preferred_element_type=F32)
             + jnp.dot(pc, vc_ref[0, j], preferred_element_type=F32))
        ol = _ones_lane(j)
        outs.append(o / o[:, ol:ol + 1])
    lane = lax.broadcasted_iota(jnp.int32, outs[0].shape, 1)
    o_ref[0] = jnp.where(lane < V_DIM, outs[0], outs[1]).astype(BF16)


def _attention(q, kx, kc, vx, vc):
    b, nh, s, hp = q.shape
    c = kc.shape[2]
    tq = Q_TILE
    kv_spec = lambda n: pl.BlockSpec((1, 2, n, hp), lambda i, h, j: (i, h, 0, 0))
    return pl.pallas_call(
        _attn_kernel,
        grid=(b, nh // 2, s // tq),
        in_specs=[pl.BlockSpec((1, 2, tq, hp), lambda i, h, j: (i, h, j, 0)),
                  kv_spec(s), kv_spec(c), kv_spec(s), kv_spec(c)],
        out_specs=pl.BlockSpec((1, tq, 2 * V_DIM), lambda i, h, j: (i, j, h)),
        out_shape=jax.ShapeDtypeStruct((b, s, nh * V_DIM), BF16),
        compiler_params=_params(("arbitrary", "arbitrary", "arbitrary")),
    )(q, kx, kc, vx, vc)


def _route(logits, run):
    tm = logits.shape[0]
    lane = lax.broadcasted_iota(jnp.int32, logits.shape, 1)
    big = jnp.int32(1 << 20)
    neg = jnp.float32(-jnp.inf)
    gmask = (lane >= N_EXPERTS) & (lane < N_EXPERTS + N_GROUPS)
    gl = jnp.where(gmask, logits, neg)
    gm = jnp.max(gl, axis=-1, keepdims=True)
    gidx = jnp.min(jnp.where(gl == gm, lane, big), axis=-1, keepdims=True) - N_EXPERTS
    gw = 1.0 / jnp.sum(jnp.where(gmask, jnp.exp(gl - gm), 0.0), axis=-1, keepdims=True)
    lo = gidx * EXPERTS_PER_GROUP
    el = jnp.where((lane >= lo) & (lane < lo + EXPERTS_PER_GROUP), logits, neg)
    m1 = jnp.max(el, axis=-1, keepdims=True)
    i1 = jnp.min(jnp.where(el == m1, lane, big), axis=-1, keepdims=True)
    el2 = jnp.where(lane == i1, neg, el)
    m2 = jnp.max(el2, axis=-1, keepdims=True)
    i2 = jnp.min(jnp.where(el2 == m2, lane, big), axis=-1, keepdims=True)
    t = jnp.exp(m2 - m1)
    w1 = gw / (1.0 + t)
    w2 = w1 * t
    hot1 = lane == i1
    hot2 = lane == i2
    onehot = jnp.where(hot1, 1.0, jnp.where(hot2, 1.0, 0.0))
    r = lax.broadcasted_iota(jnp.int32, (tm, tm), 0)
    c = lax.broadcasted_iota(jnp.int32, (tm, tm), 1)
    lower = jnp.where(c < r, 1.0, 0.0).astype(BF16)
    before = jnp.dot(lower, onehot.astype(BF16), preferred_element_type=F32) + run
    rank1 = jnp.sum(jnp.where(hot1, before, 0.0), axis=-1, keepdims=True)
    rank2 = jnp.sum(jnp.where(hot2, before, 0.0), axis=-1, keepdims=True)
    rec = jnp.where(lane == 0, i1.astype(F32),
          jnp.where(lane == 1, i2.astype(F32),
          jnp.where(lane == 2, rank1,
          jnp.where(lane == 3, rank2,
          jnp.where(lane == 4, w1,
          jnp.where(lane == 5, w2, 0.0))))))
    return rec, run + jnp.sum(onehot, axis=0, keepdims=True)


def _post_mixer(x, y, mod, n2g, wrt_ref, brt_ref, run_ref, x1_ref, h2_ref, route_ref, cnt_ref):
    @pl.when((pl.program_id(0) == 0) & (pl.program_id(1) == 0))
    def _():
        run_ref[...] = jnp.zeros_like(run_ref)

    x1 = x + mod[2:3] * y
    h2 = _rms(x1) * n2g * (1.0 + mod[4:5]) + mod[3:4]
    logits = jnp.dot(h2, wrt_ref[...], preferred_element_type=F32, precision=HIGHEST) + brt_ref[...]
    rec, run = _route(logits, run_ref[...])
    run_ref[...] = run
    cnt_ref[...] = run
    x1_ref[0] = x1
    h2_ref[0] = h2
    route_ref[0] = rec


def _router_weights(w_group, b_group, w_router, b_router):
    d = w_group.shape[0]
    wr = jnp.transpose(w_router, (1, 0, 2)).reshape(d, N_EXPERTS)
    wrt = jnp.zeros((d, LANES), F32).at[:, :N_EXPERTS].set(wr).at[:, N_EXPERTS:N_EXPERTS + N_GROUPS].set(w_group)
    brt = jnp.zeros((1, LANES), F32).at[0, :N_EXPERTS].set(b_router.reshape(-1))
    brt = brt.at[0, N_EXPERTS:N_EXPERTS + N_GROUPS].set(b_group)
    return wrt, brt


def _mixer_outs(b, s, d, tm):
    tok = lambda n: pl.BlockSpec((1, tm, n), lambda i, j: (i, j, 0))
    specs = [tok(d), tok(d), tok(LANES), _const_spec((1, LANES))]
    shapes = [jax.ShapeDtypeStruct((b, s, d), F32), jax.ShapeDtypeStruct((b, s, d), F32),
              jax.ShapeDtypeStruct((b, s, LANES), F32), jax.ShapeDtypeStruct((1, LANES), F32)]
    return specs, shapes


def _mix0_kernel(x_ref, attn_ref, z_ref, zp_ref, zn_ref, mod_ref, wdw_ref, bdw_ref, lng_ref, lnb_ref,
                 woa_ref, woc_ref, n2g_ref, wrt_ref, brt_ref,
                 x1_ref, h2_ref, route_ref, cnt_ref, run_ref, zext_ref):
    j = pl.program_id(1)
    tm = z_ref.shape[1]
    zext_ref[0:HALO] = jnp.where(j > 0, zp_ref[0].astype(F32), 0.0)
    zext_ref[HALO:HALO + tm] = z_ref[0].astype(F32)
    zext_ref[HALO + tm:] = jnp.where(j < pl.num_programs(1) - 1, zn_ref[0].astype(F32), 0.0)
    wdw = wdw_ref[...]
    acc = jnp.zeros((tm, CONV_CH), F32) + bdw_ref[...]
    base = HALO - CONV_W // 2
    for w in range(CONV_W):
        acc = acc + zext_ref[base + w:base + w + tm] * wdw[w:w + 1]
    mu = jnp.mean(acc, axis=-1, keepdims=True)
    cen = acc - mu
    var = jnp.mean(cen * cen, axis=-1, keepdims=True)
    conv = _silu(cen * lax.rsqrt(var + 1e-5) * lng_ref[...] + lnb_ref[...]).astype(BF16)
    y = (jnp.dot(attn_ref[0], woa_ref[...], preferred_element_type=F32)
         + jnp.dot(conv, woc_ref[...], preferred_element_type=F32))
    _post_mixer(x_ref[0], y, mod_ref[0], n2g_ref[...], wrt_ref, brt_ref, run_ref,
                x1_ref, h2_ref, route_ref, cnt_ref)


def _mix0(x, attn, z, mod, w_dw, b_dw, ln_g, ln_b, w_out, n2g, wrt, brt):
    b, s, d = x.shape
    tm = TOK_TILE
    nh = tm // HALO
    last = s // HALO - 1
    tok = lambda n: pl.BlockSpec((1, tm, n), lambda i, j: (i, j, 0))
    out_specs, out_shapes = _mixer_outs(b, s, d, tm)
    nattn = N_HEADS * V_DIM
    return pl.pallas_call(
        _mix0_kernel,
        grid=(b, s // tm),
        in_specs=[
            tok(d), tok(nattn), tok(CONV_CH),
            pl.BlockSpec((1, HALO, CONV_CH), lambda i, j: (i, jnp.maximum(j * nh - 1, 0), 0)),
            pl.BlockSpec((1, HALO, CONV_CH), lambda i, j: (i, jnp.minimum((j + 1) * nh, last), 0)),
            pl.BlockSpec((1, 6, d), lambda i, j: (i, 0, 0)),
            _const_spec((CONV_W, CONV_CH)), _const_spec((1, CONV_CH)), _const_spec((1, CONV_CH)),
            _const_spec((1, CONV_CH)), _const_spec((nattn, d)), _const_spec((CONV_CH, d)),
            _const_spec((1, d)), _const_spec((d, LANES)), _const_spec((1, LANES)),
        ],
        out_specs=out_specs,
        out_shape=out_shapes,
        scratch_shapes=[pltpu.VMEM((1, LANES), F32), pltpu.VMEM((tm + 2 * HALO, CONV_CH), F32)],
        compiler_params=_params(("arbitrary", "arbitrary")),
    )(x, attn, z, z, z, mod, w_dw, b_dw.reshape(1, -1), ln_g.reshape(1, -1), ln_b.reshape(1, -1),
      w_out[:nattn].astype(BF16), w_out[nattn:].astype(BF16), n2g, wrt, brt)


def _gmlp_kernel(x_ref, mod_ref, n1g_ref, win_ref, bin_ref, lng_ref, lnb_ref, ws_ref, bs_ref, wout_ref,
                 n2g_ref, wrt_ref, brt_ref, x1_ref, h2_ref, route_ref, cnt_ref, run_ref, us_ref):
    x = x_ref[0]
    mod = mod_ref[0]
    tm = x.shape[0]
    sg = lng_ref.shape[1]
    gw = sg // SG_GROUPS
    h = _modulated_norm(x, mod, n1g_ref[...], 0, 1).astype(BF16)
    u = jax.nn.gelu(jnp.dot(h, win_ref[:, :sg], preferred_element_type=F32) + bin_ref[:, :sg])
    v = jax.nn.gelu(jnp.dot(h, win_ref[:, sg:], preferred_element_type=F32) + bin_ref[:, sg:])
    mu = jnp.mean(v, axis=-1, keepdims=True)
    cen = v - mu
    var = jnp.mean(cen * cen, axis=-1, keepdims=True)
    vn = (cen * lax.rsqrt(var + 1e-5) * lng_ref[...] + lnb_ref[...]).astype(BF16)
    for c in range(tm // CHUNK):
        rows = slice(c * CHUNK, (c + 1) * CHUNK)
        for g in range(SG_GROUPS):
            cols = slice(g * gw, (g + 1) * gw)
            s = jnp.dot(ws_ref[g], vn[rows, cols], preferred_element_type=F32) + bs_ref[:, cols]
            us_ref[rows, cols] = (u[rows, cols] * s).astype(BF16)
    y = jnp.dot(us_ref[...], wout_ref[...], preferred_element_type=F32)
    _post_mixer(x, y, mod, n2g_ref[...], wrt_ref, brt_ref, run_ref, x1_ref, h2_ref, route_ref, cnt_ref)


def _gmlp(x, mod, n1g, w_in, b_in, ln_g, ln_b, w_s, b_s, w_out, n2g, wrt, brt):
    b, s, d = x.shape
    tm = TOK_TILE
    sg = w_out.shape[0]
    gw = sg // SG_GROUPS
    bs_full = jnp.repeat(b_s.T, gw, axis=1)
    out_specs, out_shapes = _mixer_outs(b, s, d, tm)
    return pl.pallas_call(
        _gmlp_kernel,
        grid=(b, s // tm),
        in_specs=[
            pl.BlockSpec((1, tm, d), lambda i, j: (i, j, 0)),
            pl.BlockSpec((1, 6, d), lambda i, j: (i, 0, 0)),
            _const_spec((1, d)), _const_spec((d, 2 * sg)), _const_spec((1, 2 * sg)),
            _const_spec((1, sg)), _const_spec((1, sg)), _const_spec((SG_GROUPS, CHUNK, CHUNK)),
            _const_spec((CHUNK, sg)), _const_spec((sg, d)),
            _const_spec((1, d)), _const_spec((d, LANES)), _const_spec((1, LANES)),
        ],
        out_specs=out_specs,
        out_shape=out_shapes,
        scratch_shapes=[pltpu.VMEM((1, LANES), F32), pltpu.VMEM((tm, sg), BF16)],
        compiler_params=_params(("arbitrary", "arbitrary")),
    )(x, mod, n1g, w_in.astype(BF16), b_in.reshape(1, -1), ln_g.reshape(1, -1), ln_b.reshape(1, -1),
      w_s.astype(BF16), bs_full, w_out.astype(BF16), n2g, wrt, brt)


def _row_copy(src_ref, src_row, dst_ref, dst_row, sem):
    return pltpu.make_async_copy(src_ref.at[pl.ds(src_row, 1), :], dst_ref.at[pl.ds(dst_row, 1), :], sem)


def _dispatch_kernel(pos_ref, h_ref, xs_in_ref, xs_ref, sem):
    del xs_in_ref
    tm = h_ref.shape[0]

    def start(t, carry):
        for k in range(2):
            _row_copy(h_ref, t, xs_ref, pos_ref[0, 0, 2 * t + k], sem).start()
        return carry

    def wait(t, carry):
        for k in range(2):
            _row_copy(h_ref, 0, xs_ref, 0, sem).wait()
        return carry

    lax.fori_loop(0, tm, start, 0)
    lax.fori_loop(0, tm, wait, 0)


def _dispatch(h2, pos3, n_rows):
    t, d = h2.shape
    tm = pos3.shape[2] // 2
    xs0 = jnp.zeros((n_rows, d), F32)
    return pl.pallas_call(
        _dispatch_kernel,
        grid=(t // tm,),
        in_specs=[
            pl.BlockSpec((1, 1, 2 * tm), lambda i: (i, 0, 0), memory_space=pltpu.SMEM),
            pl.BlockSpec((tm, d), lambda i: (i, 0)),
            pl.BlockSpec(memory_space=pl.ANY),
        ],
        out_specs=pl.BlockSpec(memory_space=pl.ANY),
        out_shape=jax.ShapeDtypeStruct((n_rows, d), F32),
        scratch_shapes=[pltpu.SemaphoreType.DMA],
        input_output_aliases={2: 0},
        compiler_params=_params(("arbitrary",)),
    )(pos3, h2, xs0)


def _expert_kernel(te_ref, nt_ref, xs_ref, wg_ref, wu_ref, wd_ref, y_ref, wgu_s, wd_s):
    i = pl.program_id(0)

    @pl.when(i < nt_ref[0])
    def _():
        prev = te_ref[jnp.maximum(i - 1, 0)]

        @pl.when((i == 0) | (te_ref[i] != prev))
        def _():
            wgu_s[:, :D_EXPERT] = wg_ref[0].astype(BF16)
            wgu_s[:, D_EXPERT:] = wu_ref[0].astype(BF16)
            wd_s[...] = wd_ref[0].astype(BF16)

        gu = jnp.dot(xs_ref[...].astype(BF16), wgu_s[...], preferred_element_type=F32)
        act = (_silu(gu[:, :D_EXPERT]) * gu[:, D_EXPERT:]).astype(BF16)
        y_ref[...] = jnp.dot(act, wd_s[...], preferred_element_type=F32)

    @pl.when(i >= nt_ref[0])
    def _():
        y_ref[...] = jnp.zeros_like(y_ref)


def _experts(xs, tile_expert, n_tiles, w_gate, w_up, w_down):
    n_rows, d = xs.shape
    tr = ROW_TILE
    grid = n_rows // tr
    row_map = lambda i, te, nt: (jnp.minimum(i, nt[0] - 1), 0)
    return pl.pallas_call(
        _expert_kernel,
        grid_spec=pltpu.PrefetchScalarGridSpec(
            num_scalar_prefetch=2,
            grid=(grid,),
            in_specs=[
                pl.BlockSpec((tr, d), row_map),
                pl.BlockSpec((1, d, D_EXPERT), lambda i, te, nt: (te[i], 0, 0)),
                pl.BlockSpec((1, d, D_EXPERT), lambda i, te, nt: (te[i], 0, 0)),
                pl.BlockSpec((1, D_EXPERT, d), lambda i, te, nt: (te[i], 0, 0)),
            ],
            out_specs=pl.BlockSpec((tr, d), lambda i, te, nt: (i, 0)),
            scratch_shapes=[pltpu.VMEM((d, 2 * D_EXPERT), BF16), pltpu.VMEM((D_EXPERT, d), BF16)],
        ),
        out_shape=jax.ShapeDtypeStruct((n_rows, d), F32),
        compiler_params=_params(("arbitrary",)),
    )(tile_expert, n_tiles, xs, w_gate, w_up, w_down)


def _combine_kernel(pos_ref, x1_ref, route_ref, mod_ref, y_ref, o_ref, g_ref, sem):
    tm = x1_ref.shape[1]

    def start(t, carry):
        for k in range(2):
            _row_copy(y_ref, pos_ref[0, 0, 2 * t + k], g_ref.at[k], t, sem).start()
        return carry

    def wait(t, carry):
        for k in range(2):
            _row_copy(y_ref, 0, g_ref.at[k], 0, sem).wait()
        return carry

    lax.fori_loop(0, tm, start, 0)
    lax.fori_loop(0, tm, wait, 0)
    rec = route_ref[0]
    moe = rec[:, 4:5] * g_ref[0] + rec[:, 5:6] * g_ref[1]
    o_ref[0] = x1_ref[0] + mod_ref[0][5:6] * moe


def _combine(x1, route, mod, y, pos3):
    b, s, d = x1.shape
    tm = pos3.shape[2] // 2
    nj = s // tm
    return pl.pallas_call(
        _combine_kernel,
        grid=(b, nj),
        in_specs=[
            pl.BlockSpec((1, 1, 2 * tm), lambda i, j: (i * nj + j, 0, 0), memory_space=pltpu.SMEM),
            pl.BlockSpec((1, tm, d), lambda i, j: (i, j, 0)),
            pl.BlockSpec((1, tm, LANES), lambda i, j: (i, j, 0)),
            pl.BlockSpec((1, 6, d), lambda i, j: (i, 0, 0)),
            pl.BlockSpec(memory_space=pl.ANY),
        ],
        out_specs=pl.BlockSpec((1, tm, d), lambda i, j: (i, j, 0)),
        out_shape=jax.ShapeDtypeStruct((b, s, d), F32),
        scratch_shapes=[pltpu.VMEM((2, tm, d), F32), pltpu.SemaphoreType.DMA],
        compiler_params=_params(("arbitrary", "arbitrary")),
    )(pos3, x1, route, mod, y)


def _moe(x1, h2, route, counts, mod, w_gate, w_up, w_down):
    b, s, d = x1.shape
    t = b * s
    tr = ROW_TILE
    tm = TOK_TILE
    rec = route.reshape(t, LANES)
    eid = rec[:, 0:2].astype(jnp.int32)
    rank = rec[:, 2:4].astype(jnp.int32)
    cnt = counts[0, :N_EXPERTS].astype(jnp.int32)
    padded = (cnt + tr - 1) // tr * tr
    ends = jnp.cumsum(padded)
    pos = (ends - padded)[eid] + rank
    pos3 = pos.reshape(t // tm, 1, 2 * tm)
    max_tiles = 2 * t // tr + N_EXPERTS
    n_tiles = (ends[-1] // tr).astype(jnp.int32)
    tile_ids = jnp.arange(max_tiles, dtype=jnp.int32)
    last_tile = jnp.minimum(tile_ids, n_tiles - 1)
    tile_expert = jnp.sum((ends // tr)[None, :] <= last_tile[:, None], axis=1)
    tile_expert = jnp.minimum(tile_expert, N_EXPERTS - 1).astype(jnp.int32)
    xs = _dispatch(h2.reshape(t, d), pos3, max_tiles * tr)
    y = _experts(xs, tile_expert, n_tiles.reshape(1), w_gate, w_up, w_down)
    return _combine(x1, route, mod, y, pos3)


def kernel(x, c, ctx, c_ctx, w_ada, b_ada, norm1_g, norm2_g, a_w_in, a_q_norm_g, a_w_uq, a_kv_norm_g, a_w_ukv,
           a_q_g, a_k_g, b_w_dw, b_b_dw, b_ln_g, b_ln_b, ab_w_out, c_w_in, c_b_in, c_ln_g, c_ln_b, c_w_s, c_b_s,
           c_w_out, moe_w_group, moe_b_group, moe_w_router, moe_b_router, moe_w_gate, moe_w_up, moe_w_down):
    b, s, d = x.shape
    rows = -(-(b + 1) // 8) * 8
    cvec = jnp.zeros((rows, d), F32).at[:b].set(c).at[b].set(c_ctx)
    mod_all = _ada(cvec, w_ada, b_ada)
    mods = [mod_all[l, :b].reshape(b, 6, d) for l in range(w_ada.shape[0])]
    mod_ctx = mod_all[0, b].reshape(1, 6, d)
    row = lambda v: v.reshape(1, -1)

    win_lat, win_ctx, wuq, wuk, wuv, vones, qg, kg = _mla_weights(a_w_in[0], a_w_uq[0], a_w_ukv[0], a_q_g[0], a_k_g[0])
    n1g = row(norm1_g[0])
    kvng = row(a_kv_norm_g[0])
    q, kx, vx, z = _mla_proj(x, mods[0], n1g, win_lat, row(a_q_norm_g[0]), wuq, kvng, wuk, wuv, vones, qg, kg,
                             _rope_tables(s))
    kc, vc = _ctx_proj(ctx, mod_ctx, n1g, win_ctx, kvng, wuk, wuv, vones, kg)
    attn = _attention(q, kx, kc, vx, vc)
    wrt, brt = _router_weights(moe_w_group[0], moe_b_group[0], moe_w_router[0], moe_b_router[0])
    x1, h2, route, counts = _mix0(x, attn, z, mods[0], b_w_dw[0], b_b_dw[0], b_ln_g[0], b_ln_b[0], ab_w_out[0],
                                  row(norm2_g[0]), wrt, brt)
    x = _moe(x1, h2, route, counts, mods[0], moe_w_gate[0], moe_w_up[0], moe_w_down[0])

    wrt, brt = _router_weights(moe_w_group[1], moe_b_group[1], moe_w_router[1], moe_b_router[1])
    x1, h2, route, counts = _gmlp(x, mods[1], row(norm1_g[1]), c_w_in[0], c_b_in[0], c_ln_g[0], c_ln_b[0],
                                  c_w_s[0], c_b_s[0], c_w_out[0], row(norm2_g[1]), wrt, brt)
    return _moe(x1, h2, route, counts, mods[1], moe_w_gate[1], moe_w_up[1], moe_w_down[1])
```

```python
import functools
import math

import jax
import jax.numpy as jnp
from jax import lax
from jax.experimental import pallas as pl
from jax.experimental.pallas import tpu as pltpu
from jax.experimental.pallas import tpu_sc as plsc

F32 = jnp.float32
BF16 = jnp.bfloat16
HIGHEST = lax.Precision.HIGHEST

N_HEADS = 8
Q_LORA = 384
KV_LORA = 256
NOPE_DIM = 64
ROPE_DIM = 32
QK_DIM = NOPE_DIM + ROPE_DIM
V_DIM = 64
ROPE_THETA = 10000.0
GRID_W = 64
CONV_CH = 512
CONV_W = 31
CHUNK = 128
SG_GROUPS = 8
N_GROUPS = 4
EXPERTS_PER_GROUP = 8
N_EXPERTS = N_GROUPS * EXPERTS_PER_GROUP
D_EXPERT = 256

LANES = 128
PACK_CHUNKS = 4
SC_CORES = 2
SC_SUBCORES = 16
SC_ROWS = 128
HEAD_PAD = LANES
HALO = 16

TOK_TILE = 256
Q_TILE = 256
ROW_TILE = 256
VMEM_LIMIT = 56 << 20


def _params(sem, vmem=VMEM_LIMIT):
    return pltpu.CompilerParams(dimension_semantics=sem, vmem_limit_bytes=vmem)


def _const_spec(shape):
    nd = len(shape)
    return pl.BlockSpec(shape, lambda *_: (0,) * nd)


def _rms(x, eps=1e-6):
    return x * lax.rsqrt(jnp.mean(x * x, axis=-1, keepdims=True) + eps)


def _silu(x):
    return x * jax.nn.sigmoid(x)


def _bf16_bits(x):
    u = pltpu.bitcast(x, jnp.uint32)
    return u + jnp.uint32(0x7FFF) + ((u >> 16) & jnp.uint32(1))


def _pack_rows(x, ref):
    n = x.shape[1] // 2
    w = (_bf16_bits(x[:, :n]) >> 16) | (_bf16_bits(x[:, n:]) & jnp.uint32(0xFFFF0000))
    for c in range(PACK_CHUNKS):
        ref[c] = w[:, c * LANES:(c + 1) * LANES]


def _unpack_rows(ref):
    w = jnp.concatenate([ref[c] for c in range(PACK_CHUNKS)], axis=1)
    return pltpu.bitcast(w << 16, F32), pltpu.bitcast(w & jnp.uint32(0xFFFF0000), F32)


def _ada_kernel(c_ref, w_ref, b_ref, o_ref):
    sc = _silu(c_ref[...])
    o_ref[0] = jnp.dot(sc, w_ref[0], preferred_element_type=F32, precision=HIGHEST) + b_ref[0]


def _ada(cvec, w_ada, b_ada):
    depth, d, n = w_ada.shape
    rows = cvec.shape[0]
    tn = 1536
    return pl.pallas_call(
        _ada_kernel,
        grid=(depth, n // tn),
        in_specs=[
            pl.BlockSpec((rows, d), lambda l, j: (0, 0)),
            pl.BlockSpec((1, d, tn), lambda l, j: (l, 0, j)),
            pl.BlockSpec((1, 1, tn), lambda l, j: (l, 0, j)),
        ],
        out_specs=pl.BlockSpec((1, rows, tn), lambda l, j: (l, 0, j)),
        out_shape=jax.ShapeDtypeStruct((depth, rows, n), F32),
        compiler_params=_params(("arbitrary", "arbitrary")),
    )(cvec, w_ada, b_ada.reshape(depth, 1, n))


def _rope(t, rc, rs1, rs2):
    return t * rc + pltpu.roll(t, HEAD_PAD - ROPE_DIM // 2, 1) * rs1 + pltpu.roll(t, ROPE_DIM // 2, 1) * rs2


def _head_norm(t, gain):
    ss = jnp.sum(t * t, axis=-1, keepdims=True) * (1.0 / QK_DIM)
    return t * lax.rsqrt(ss + 1e-6) * gain


def _kv_heads(kvl, kr, kvng, wuk_ref, wuv_ref, vones, kg, rope, k_ref, v_ref):
    kvn = (_rms(kvl) * kvng).astype(BF16)
    kfull = jnp.dot(kvn, wuk_ref[...], preferred_element_type=F32)
    vfull = jnp.dot(kvn, wuv_ref[...], preferred_element_type=F32) + vones
    for h in range(N_HEADS):
        sl = slice(h * HEAD_PAD, (h + 1) * HEAD_PAD)
        kn = _head_norm(kfull[:, sl] + kr, kg)
        if rope is not None:
            kn = _rope(kn, *rope)
        k_ref[0, h] = kn.astype(BF16)
        v_ref[0, h] = vfull[:, sl].astype(BF16)


def _modulated_norm(x, mod, gain, shift_row, scale_row):
    return _rms(x) * gain * (1.0 + mod[scale_row:scale_row + 1]) + mod[shift_row:shift_row + 1]


def _mla_proj_kernel(x_ref, mod_ref, n1g_ref, win_ref, qng_ref, wuq_ref, kvng_ref, wuk_ref, wuv_ref,
                     vones_ref, qg_ref, kg_ref, rc_ref, rs1_ref, rs2_ref,
                     q_ref, k_ref, v_ref, z_ref):
    h = _modulated_norm(x_ref[0], mod_ref[0], n1g_ref[...], 0, 1).astype(BF16)
    proj = jnp.dot(h, win_ref[...], preferred_element_type=F32)
    o_kv = Q_LORA
    o_a = o_kv + KV_LORA
    o_b = o_a + CONV_CH
    o_kr = o_b + CONV_CH
    rope = (rc_ref[...], rs1_ref[...], rs2_ref[...])
    qn = (_rms(proj[:, :o_kv]) * qng_ref[...]).astype(BF16)
    qfull = jnp.dot(qn, wuq_ref[...], preferred_element_type=F32)
    qg = qg_ref[...]
    for hd in range(N_HEADS):
        sl = slice(hd * HEAD_PAD, (hd + 1) * HEAD_PAD)
        q_ref[0, hd] = _rope(_head_norm(qfull[:, sl], qg), *rope).astype(BF16)
    _kv_heads(proj[:, o_kv:o_a], proj[:, o_kr:o_kr + HEAD_PAD], kvng_ref[...], wuk_ref, wuv_ref,
              vones_ref[...], kg_ref[...], rope, k_ref, v_ref)
    z_ref[0] = (proj[:, o_a:o_b] * jax.nn.sigmoid(proj[:, o_b:o_kr])).astype(BF16)


def _ctx_proj_kernel(x_ref, mod_ref, n1g_ref, win_ref, kvng_ref, wuk_ref, wuv_ref, vones_ref, kg_ref,
                     k_ref, v_ref):
    h = _modulated_norm(x_ref[0], mod_ref[0], n1g_ref[...], 0, 1).astype(BF16)
    proj = jnp.dot(h, win_ref[...], preferred_element_type=F32)
    _kv_heads(proj[:, :KV_LORA], proj[:, KV_LORA:KV_LORA + HEAD_PAD], kvng_ref[...], wuk_ref, wuv_ref,
              vones_ref[...], kg_ref[...], None, k_ref, v_ref)


def _rope_tables(s):
    t = jnp.arange(s)
    row = (t // GRID_W).astype(F32)
    col = (t % GRID_W).astype(F32)
    n_freq = ROPE_DIM // 4
    inv_freq = ROPE_THETA ** (-jnp.arange(n_freq, dtype=F32) / n_freq)
    ang = jnp.concatenate([row[:, None] * inv_freq, col[:, None] * inv_freq], axis=-1)
    cos, sin = jnp.cos(ang), jnp.sin(ang)
    half = ROPE_DIM // 2
    zeros = jnp.zeros((s, half), F32)
    tail = jnp.zeros((s, HEAD_PAD - QK_DIM), F32)
    rc = jnp.concatenate([jnp.ones((s, NOPE_DIM), F32), cos, cos, tail], axis=-1)
    rs1 = jnp.concatenate([jnp.zeros((s, NOPE_DIM), F32), -sin, zeros, tail], axis=-1)
    rs2 = jnp.concatenate([jnp.zeros((s, NOPE_DIM), F32), zeros, sin, tail], axis=-1)
    return rc, rs1, rs2


def _pad_heads(w, left=0):
    k, nh, width = w.shape
    return jnp.pad(w, ((0, 0), (0, 0), (left, HEAD_PAD - width - left))).reshape(k, nh * HEAD_PAD)


def _ones_lane(h):
    return V_DIM if h % 2 == 0 else 0


def _mla_weights(w_in, w_uq, w_ukv, q_g, k_g):
    d = w_in.shape[0]
    o1, o2, o3 = Q_LORA, Q_LORA + KV_LORA, Q_LORA + KV_LORA + ROPE_DIM
    kr_block = jnp.pad(w_in[:, o2:o3], ((0, 0), (NOPE_DIM, HEAD_PAD - QK_DIM)))
    glu = w_in[:, o3:]
    win_lat = jnp.concatenate([w_in[:, :o1], w_in[:, o1:o2], glu, kr_block], axis=1).astype(BF16)
    win_ctx = jnp.concatenate([w_in[:, o1:o2], kr_block], axis=1).astype(BF16)
    wuq = _pad_heads(w_uq.reshape(Q_LORA, N_HEADS, QK_DIM)).astype(BF16)
    ukv = w_ukv.reshape(KV_LORA, N_HEADS, NOPE_DIM + V_DIM)
    wuk = _pad_heads(ukv[:, :, :NOPE_DIM]).astype(BF16)
    uv = ukv[:, :, NOPE_DIM:].reshape(KV_LORA, N_HEADS // 2, 2, V_DIM)
    wuv = jnp.stack([jnp.pad(uv[:, :, 0], ((0, 0), (0, 0), (0, HEAD_PAD - V_DIM))),
                     jnp.pad(uv[:, :, 1], ((0, 0), (0, 0), (HEAD_PAD - V_DIM, 0)))], axis=2)
    wuv = wuv.reshape(KV_LORA, N_HEADS * HEAD_PAD).astype(BF16)
    lane = jnp.arange(N_HEADS * HEAD_PAD)
    ones_at = jnp.where((lane // HEAD_PAD) % 2 == 0, V_DIM, 0)
    vones = (lane % HEAD_PAD == ones_at).astype(F32).reshape(1, N_HEADS * HEAD_PAD)
    pad = jnp.zeros((HEAD_PAD - QK_DIM,), F32)
    qg = (jnp.concatenate([q_g, pad]) * (QK_DIM ** -0.5 * math.log2(math.e))).reshape(1, HEAD_PAD)
    kg = jnp.concatenate([k_g, pad]).reshape(1, HEAD_PAD)
    return win_lat, win_ctx, wuq, wuk, wuv, vones, qg, kg


def _mla_proj(x, mod, n1g, win, qng, wuq, kvng, wuk, wuv, vones, qg, kg, rope):
    b, s, d = x.shape
    tm = TOK_TILE
    hp = N_HEADS * HEAD_PAD
    head_out = jax.ShapeDtypeStruct((b, N_HEADS, s, HEAD_PAD), BF16)
    head_spec = pl.BlockSpec((1, N_HEADS, tm, HEAD_PAD), lambda i, j: (i, 0, j, 0))
    rope_spec = pl.BlockSpec((tm, HEAD_PAD), lambda i, j: (j, 0))
    return pl.pallas_call(
        _mla_proj_kernel,
        grid=(b, s // tm),
        in_specs=[
            pl.BlockSpec((1, tm, d), lambda i, j: (i, j, 0)),
            pl.BlockSpec((1, 6, d), lambda i, j: (i, 0, 0)),
            _const_spec((1, d)), _const_spec(win.shape), _const_spec((1, Q_LORA)), _const_spec(wuq.shape),
            _const_spec((1, KV_LORA)), _const_spec(wuk.shape), _const_spec(wuv.shape), _const_spec((1, hp)),
            _const_spec((1, HEAD_PAD)), _const_spec((1, HEAD_PAD)), rope_spec, rope_spec, rope_spec,
        ],
        out_specs=[head_spec, head_spec, head_spec, pl.BlockSpec((1, tm, CONV_CH), lambda i, j: (i, j, 0))],
        out_shape=[head_out, head_out, head_out, jax.ShapeDtypeStruct((b, s, CONV_CH), BF16)],
        compiler_params=_params(("arbitrary", "arbitrary")),
    )(x, mod, n1g, win, qng, wuq, kvng, wuk, wuv, vones, qg, kg, *rope)


def _ctx_proj(ctx, mod_c, n1g, win, kvng, wuk, wuv, vones, kg):
    b, s, d = ctx.shape
    tm = min(TOK_TILE, s)
    hp = N_HEADS * HEAD_PAD
    head_out = jax.ShapeDtypeStruct((b, N_HEADS, s, HEAD_PAD), BF16)
    head_spec = pl.BlockSpec((1, N_HEADS, tm, HEAD_PAD), lambda i, j: (i, 0, j, 0))
    return pl.pallas_call(
        _ctx_proj_kernel,
        grid=(b, s // tm),
        in_specs=[
            pl.BlockSpec((1, tm, d), lambda i, j: (i, j, 0)),
            _const_spec((1, 6, d)),
            _const_spec((1, d)), _const_spec(win.shape), _const_spec((1, KV_LORA)), _const_spec(wuk.shape),
            _const_spec(wuv.shape), _const_spec((1, hp)), _const_spec((1, HEAD_PAD)),
        ],
        out_specs=[head_spec, head_spec],
        out_shape=[head_out, head_out],
        compiler_params=_params(("arbitrary", "arbitrary")),
    )(ctx, mod_c, n1g, win, kvng, wuk, wuv, vones, kg)


def _attn_kernel(q_ref, kx_ref, kc_ref, vx_ref, vc_ref, o_ref):
    nt = (((1,), (1,)), ((), ()))
    outs = []
    for j in range(2):
        q = q_ref[0, j]
        sx = lax.dot_general(q, kx_ref[0, j], nt, preferred_element_type=F32)
        sc = lax.dot_general(q, kc_ref[0, j], nt, preferred_element_type=F32)
        m = jnp.maximum(jnp.max(sx, axis=-1, keepdims=True), jnp.max(sc, axis=-1, keepdims=True))
        px = jnp.exp2(sx - m).astype(BF16)
        pc = jnp.exp2(sc - m).astype(BF16)
        o = (jnp.dot(px, vx_ref[0, j], preferred_element_type=F32)
             + jnp.dot(pc, vc_ref[0, j], preferred_element_type=F32))
        ol = _ones_lane(j)
        outs.append(o / o[:, ol:ol + 1])
    lane = lax.broadcasted_iota(jnp.int32, outs[0].shape, 1)
    o_ref[0] = jnp.where(lane < V_DIM, outs[0], outs[1]).astype(BF16)


def _attention(q, kx, kc, vx, vc):
    b, nh, s, hp = q.shape
    c = kc.shape[2]
    tq = Q_TILE
    kv_spec = lambda n: pl.BlockSpec((1, 2, n, hp), lambda i, h, j: (i, h, 0, 0))
    return pl.pallas_call(
        _attn_kernel,
        grid=(b, nh // 2, s // tq),
        in_specs=[pl.BlockSpec((1, 2, tq, hp), lambda i, h, j: (i, h, j, 0)),
                  kv_spec(s), kv_spec(c), kv_spec(s), kv_spec(c)],
        out_specs=pl.BlockSpec((1, tq, 2 * V_DIM), lambda i, h, j: (i, j, h)),
        out_shape=jax.ShapeDtypeStruct((b, s, nh * V_DIM), BF16),
        compiler_params=_params(("arbitrary", "arbitrary", "arbitrary")),
    )(q, kx, kc, vx, vc)


def _route(logits, run):
    tm = logits.shape[0]
    lane = lax.broadcasted_iota(jnp.int32, logits.shape, 1)
    big = jnp.int32(1 << 20)
    neg = jnp.float32(-jnp.inf)
    gmask = (lane >= N_EXPERTS) & (lane < N_EXPERTS + N_GROUPS)
    gl = jnp.where(gmask, logits, neg)
    gm = jnp.max(gl, axis=-1, keepdims=True)
    gidx = jnp.min(jnp.where(gl == gm, lane, big), axis=-1, keepdims=True) - N_EXPERTS
    gw = 1.0 / jnp.sum(jnp.where(gmask, jnp.exp(gl - gm), 0.0), axis=-1, keepdims=True)
    lo = gidx * EXPERTS_PER_GROUP
    el = jnp.where((lane >= lo) & (lane < lo + EXPERTS_PER_GROUP), logits, neg)
    m1 = jnp.max(el, axis=-1, keepdims=True)
    i1 = jnp.min(jnp.where(el == m1, lane, big), axis=-1, keepdims=True)
    el2 = jnp.where(lane == i1, neg, el)
    m2 = jnp.max(el2, axis=-1, keepdims=True)
    i2 = jnp.min(jnp.where(el2 == m2, lane, big), axis=-1, keepdims=True)
    t = jnp.exp(m2 - m1)
    w1 = gw / (1.0 + t)
    w2 = w1 * t
    hot1 = lane == i1
    hot2 = lane == i2
    onehot = jnp.where(hot1, 1.0, jnp.where(hot2, 1.0, 0.0))
    r = lax.broadcasted_iota(jnp.int32, (tm, tm), 0)
    c = lax.broadcasted_iota(jnp.int32, (tm, tm), 1)
    lower = jnp.where(c < r, 1.0, 0.0).astype(BF16)
    before = jnp.dot(lower, onehot.astype(BF16), preferred_element_type=F32) + run
    rank1 = jnp.sum(jnp.where(hot1, before, 0.0), axis=-1, keepdims=True)
    rank2 = jnp.sum(jnp.where(hot2, before, 0.0), axis=-1, keepdims=True)
    rec = jnp.where(lane == 0, i1.astype(F32),
          jnp.where(lane == 1, i2.astype(F32),
          jnp.where(lane == 2, rank1,
          jnp.where(lane == 3, rank2,
          jnp.where(lane == 4, w1,
          jnp.where(lane == 5, w2, 0.0))))))
    return rec, run + jnp.sum(onehot, axis=0, keepdims=True)


def _post_mixer(x, y, mod, n2g, wrt_ref, brt_ref, run_ref, x1_ref, h2_ref, route_ref, cnt_ref):
    @pl.when((pl.program_id(0) == 0) & (pl.program_id(1) == 0))
    def _():
        run_ref[...] = jnp.zeros_like(run_ref)

    x1 = x + mod[2:3] * y
    h2 = _rms(x1) * n2g * (1.0 + mod[4:5]) + mod[3:4]
    logits = jnp.dot(h2, wrt_ref[...], preferred_element_type=F32, precision=HIGHEST) + brt_ref[...]
    rec, run = _route(logits, run_ref[...])
    run_ref[...] = run
    cnt_ref[...] = run
    x1_ref[0] = x1
    _pack_rows(h2, h2_ref)
    route_ref[0] = rec


def _router_weights(w_group, b_group, w_router, b_router):
    d = w_group.shape[0]
    wr = jnp.transpose(w_router, (1, 0, 2)).reshape(d, N_EXPERTS)
    wrt = jnp.zeros((d, LANES), F32).at[:, :N_EXPERTS].set(wr).at[:, N_EXPERTS:N_EXPERTS + N_GROUPS].set(w_group)
    brt = jnp.zeros((1, LANES), F32).at[0, :N_EXPERTS].set(b_router.reshape(-1))
    brt = brt.at[0, N_EXPERTS:N_EXPERTS + N_GROUPS].set(b_group)
    return wrt, brt


def _mixer_outs(b, s, d, tm):
    tok = lambda n: pl.BlockSpec((1, tm, n), lambda i, j: (i, j, 0))
    nj = s // tm
    packed = pl.BlockSpec((PACK_CHUNKS, tm, LANES), lambda i, j: (0, i * nj + j, 0))
    specs = [tok(d), packed, tok(LANES), _const_spec((1, LANES))]
    shapes = [jax.ShapeDtypeStruct((b, s, d), F32), jax.ShapeDtypeStruct((PACK_CHUNKS, b * s, LANES), jnp.uint32),
              jax.ShapeDtypeStruct((b, s, LANES), F32), jax.ShapeDtypeStruct((1, LANES), F32)]
    return specs, shapes


def _mix0_kernel(x_ref, attn_ref, z_ref, zp_ref, zn_ref, mod_ref, wdw_ref, bdw_ref, lng_ref, lnb_ref,
                 woa_ref, woc_ref, n2g_ref, wrt_ref, brt_ref,
                 x1_ref, h2_ref, route_ref, cnt_ref, run_ref, zext_ref):
    j = pl.program_id(1)
    tm = z_ref.shape[1]
    zext_ref[0:HALO] = jnp.where(j > 0, zp_ref[0].astype(F32), 0.0)
    zext_ref[HALO:HALO + tm] = z_ref[0].astype(F32)
    zext_ref[HALO + tm:] = jnp.where(j < pl.num_programs(1) - 1, zn_ref[0].astype(F32), 0.0)
    wdw = wdw_ref[...]
    acc = jnp.zeros((tm, CONV_CH), F32) + bdw_ref[...]
    base = HALO - CONV_W // 2
    for w in range(CONV_W):
        acc = acc + zext_ref[base + w:base + w + tm] * wdw[w:w + 1]
    mu = jnp.mean(acc, axis=-1, keepdims=True)
    cen = acc - mu
    var = jnp.mean(cen * cen, axis=-1, keepdims=True)
    conv = _silu(cen * lax.rsqrt(var + 1e-5) * lng_ref[...] + lnb_ref[...]).astype(BF16)
    y = (jnp.dot(attn_ref[0], woa_ref[...], preferred_element_type=F32)
         + jnp.dot(conv, woc_ref[...], preferred_element_type=F32))
    _post_mixer(x_ref[0], y, mod_ref[0], n2g_ref[...], wrt_ref, brt_ref, run_ref,
                x1_ref, h2_ref, route_ref, cnt_ref)


def _mix0(x, attn, z, mod, w_dw, b_dw, ln_g, ln_b, w_out, n2g, wrt, brt):
    b, s, d = x.shape
    tm = TOK_TILE
    nh = tm // HALO
    last = s // HALO - 1
    tok = lambda n: pl.BlockSpec((1, tm, n), lambda i, j: (i, j, 0))
    out_specs, out_shapes = _mixer_outs(b, s, d, tm)
    nattn = N_HEADS * V_DIM
    return pl.pallas_call(
        _mix0_kernel,
        grid=(b, s // tm),
        in_specs=[
            tok(d), tok(nattn), tok(CONV_CH),
            pl.BlockSpec((1, HALO, CONV_CH), lambda i, j: (i, jnp.maximum(j * nh - 1, 0), 0)),
            pl.BlockSpec((1, HALO, CONV_CH), lambda i, j: (i, jnp.minimum((j + 1) * nh, last), 0)),
            pl.BlockSpec((1, 6, d), lambda i, j: (i, 0, 0)),
            _const_spec((CONV_W, CONV_CH)), _const_spec((1, CONV_CH)), _const_spec((1, CONV_CH)),
            _const_spec((1, CONV_CH)), _const_spec((nattn, d)), _const_spec((CONV_CH, d)),
            _const_spec((1, d)), _const_spec((d, LANES)), _const_spec((1, LANES)),
        ],
        out_specs=out_specs,
        out_shape=out_shapes,
        scratch_shapes=[pltpu.VMEM((1, LANES), F32), pltpu.VMEM((tm + 2 * HALO, CONV_CH), F32)],
        compiler_params=_params(("arbitrary", "arbitrary")),
    )(x, attn, z, z, z, mod, w_dw, b_dw.reshape(1, -1), ln_g.reshape(1, -1), ln_b.reshape(1, -1),
      w_out[:nattn].astype(BF16), w_out[nattn:].astype(BF16), n2g, wrt, brt)


def _gmlp_kernel(x_ref, mod_ref, n1g_ref, win_ref, bin_ref, lng_ref, lnb_ref, ws_ref, bs_ref, wout_ref,
                 n2g_ref, wrt_ref, brt_ref, x1_ref, h2_ref, route_ref, cnt_ref, run_ref, us_ref):
    x = x_ref[0]
    mod = mod_ref[0]
    tm = x.shape[0]
    sg = lng_ref.shape[1]
    gw = sg // SG_GROUPS
    h = _modulated_norm(x, mod, n1g_ref[...], 0, 1).astype(BF16)
    u = jax.nn.gelu(jnp.dot(h, win_ref[:, :sg], preferred_element_type=F32) + bin_ref[:, :sg])
    v = jax.nn.gelu(jnp.dot(h, win_ref[:, sg:], preferred_element_type=F32) + bin_ref[:, sg:])
    mu = jnp.mean(v, axis=-1, keepdims=True)
    cen = v - mu
    var = jnp.mean(cen * cen, axis=-1, keepdims=True)
    vn = (cen * lax.rsqrt(var + 1e-5) * lng_ref[...] + lnb_ref[...]).astype(BF16)
    for c in range(tm // CHUNK):
        rows = slice(c * CHUNK, (c + 1) * CHUNK)
        for g in range(SG_GROUPS):
            cols = slice(g * gw, (g + 1) * gw)
            s = jnp.dot(ws_ref[g], vn[rows, cols], preferred_element_type=F32) + bs_ref[:, cols]
            us_ref[rows, cols] = (u[rows, cols] * s).astype(BF16)
    y = jnp.dot(us_ref[...], wout_ref[...], preferred_element_type=F32)
    _post_mixer(x, y, mod, n2g_ref[...], wrt_ref, brt_ref, run_ref, x1_ref, h2_ref, route_ref, cnt_ref)


def _gmlp(x, mod, n1g, w_in, b_in, ln_g, ln_b, w_s, b_s, w_out, n2g, wrt, brt):
    b, s, d = x.shape
    tm = TOK_TILE
    sg = w_out.shape[0]
    gw = sg // SG_GROUPS
    bs_full = jnp.repeat(b_s.T, gw, axis=1)
    out_specs, out_shapes = _mixer_outs(b, s, d, tm)
    return pl.pallas_call(
        _gmlp_kernel,
        grid=(b, s // tm),
        in_specs=[
            pl.BlockSpec((1, tm, d), lambda i, j: (i, j, 0)),
            pl.BlockSpec((1, 6, d), lambda i, j: (i, 0, 0)),
            _const_spec((1, d)), _const_spec((d, 2 * sg)), _const_spec((1, 2 * sg)),
            _const_spec((1, sg)), _const_spec((1, sg)), _const_spec((SG_GROUPS, CHUNK, CHUNK)),
            _const_spec((CHUNK, sg)), _const_spec((sg, d)),
            _const_spec((1, d)), _const_spec((d, LANES)), _const_spec((1, LANES)),
        ],
        out_specs=out_specs,
        out_shape=out_shapes,
        scratch_shapes=[pltpu.VMEM((1, LANES), F32), pltpu.VMEM((tm, sg), BF16)],
        compiler_params=_params(("arbitrary", "arbitrary")),
    )(x, mod, n1g, w_in.astype(BF16), b_in.reshape(1, -1), ln_g.reshape(1, -1), ln_b.reshape(1, -1),
      w_s.astype(BF16), bs_full, w_out.astype(BF16), n2g, wrt, brt)


def _sc_mesh():
    return plsc.VectorSubcoreMesh(core_axis_name="c", subcore_axis_name="s",
                                  num_cores=SC_CORES, num_subcores=SC_SUBCORES)


def _sc_worker_base(rows_per_worker):
    return (lax.axis_index("s") * SC_CORES + lax.axis_index("c")) * rows_per_worker


def _sc_scatter_rows(rows, idx0, idx1, n_out):
    nc, t, w = rows.shape
    per = t // (SC_CORES * SC_SUBCORES)
    cb = SC_ROWS

    def body(rows_hbm, i0_hbm, i1_hbm, out_hbm, i0_v, i1_v, rows_v, sem):
        base0 = _sc_worker_base(per)

        @pl.loop(0, per // cb)
        def _(j):
            base = pl.multiple_of(base0 + j * cb, cb)
            pltpu.sync_copy(i0_hbm.at[pl.ds(base, cb)], i0_v)
            pltpu.sync_copy(i1_hbm.at[pl.ds(base, cb)], i1_v)
            for c in range(nc):
                pltpu.sync_copy(rows_hbm.at[c, pl.ds(base, cb)], rows_v.at[c])
            copies = []
            for c in range(nc):
                copies.append(pltpu.async_copy(rows_v.at[c], out_hbm.at[c].at[i0_v], sem))
                copies.append(pltpu.async_copy(rows_v.at[c], out_hbm.at[c].at[i1_v], sem))
            for cp in copies:
                cp.wait()

    return pl.kernel(
        body,
        out_type=jax.ShapeDtypeStruct((nc, n_out, w), rows.dtype),
        mesh=_sc_mesh(),
        scratch_types=[pltpu.VMEM((cb,), jnp.int32), pltpu.VMEM((cb,), jnp.int32),
                       pltpu.VMEM((nc, cb, w), rows.dtype), pltpu.SemaphoreType.DMA],
    )(rows, idx0, idx1)


def _sc_gather_rows(table, idx0, idx1):
    nc, _, w = table.shape
    t = idx0.shape[0]
    per = t // (SC_CORES * SC_SUBCORES)
    cb = SC_ROWS

    def body(table_hbm, i0_hbm, i1_hbm, out_hbm, idx_v, rows_v, sem):
        base0 = _sc_worker_base(per)

        @pl.loop(0, per // cb)
        def _(j):
            base = pl.multiple_of(base0 + j * cb, cb)
            for k, idx_hbm in enumerate((i0_hbm, i1_hbm)):
                pltpu.sync_copy(idx_hbm.at[pl.ds(base, cb)], idx_v)
                copies = [pltpu.async_copy(table_hbm.at[c].at[idx_v], rows_v.at[c], sem) for c in range(nc)]
                for cp in copies:
                    cp.wait()
                for c in range(nc):
                    pltpu.sync_copy(rows_v.at[c], out_hbm.at[k, c, pl.ds(base, cb)])

    return pl.kernel(
        body,
        out_type=jax.ShapeDtypeStruct((2, nc, t, w), table.dtype),
        mesh=_sc_mesh(),
        scratch_types=[pltpu.VMEM((cb,), jnp.int32), pltpu.VMEM((nc, cb, w), table.dtype),
                       pltpu.SemaphoreType.DMA],
    )(table, idx0, idx1)


def _expert_kernel(te_ref, nt_ref, xs_ref, wg_ref, wu_ref, wd_ref, y_ref, wgu_s, wd_s):
    i = pl.program_id(0)
    half = wgu_s.shape[0] // 2

    @pl.when(i < nt_ref[0])
    def _():
        prev = te_ref[jnp.maximum(i - 1, 0)]

        @pl.when((i == 0) | (te_ref[i] != prev))
        def _():
            wgu_s[:, :D_EXPERT] = wg_ref[0].astype(BF16)
            wgu_s[:, D_EXPERT:] = wu_ref[0].astype(BF16)
            wd_s[...] = wd_ref[0].astype(BF16)

        lo, hi = _unpack_rows(xs_ref)
        gu = (jnp.dot(lo.astype(BF16), wgu_s[:half], preferred_element_type=F32)
              + jnp.dot(hi.astype(BF16), wgu_s[half:], preferred_element_type=F32))
        act = (_silu(gu[:, :D_EXPERT]) * gu[:, D_EXPERT:]).astype(BF16)
        _pack_rows(jnp.dot(act, wd_s[...], preferred_element_type=F32), y_ref)

    @pl.when(i >= nt_ref[0])
    def _():
        y_ref[...] = jnp.zeros_like(y_ref)


def _experts(xs, tile_expert, n_tiles, w_gate, w_up, w_down):
    nc, n_rows, w = xs.shape
    d = w_gate.shape[1]
    tr = ROW_TILE
    row_map = lambda i, te, nt: (0, jnp.minimum(i, nt[0] - 1), 0)
    return pl.pallas_call(
        _expert_kernel,
        grid_spec=pltpu.PrefetchScalarGridSpec(
            num_scalar_prefetch=2,
            grid=(n_rows // tr,),
            in_specs=[
                pl.BlockSpec((nc, tr, w), row_map),
                pl.BlockSpec((1, d, D_EXPERT), lambda i, te, nt: (te[i], 0, 0)),
                pl.BlockSpec((1, d, D_EXPERT), lambda i, te, nt: (te[i], 0, 0)),
                pl.BlockSpec((1, D_EXPERT, d), lambda i, te, nt: (te[i], 0, 0)),
            ],
            out_specs=pl.BlockSpec((nc, tr, w), lambda i, te, nt: (0, i, 0)),
            scratch_shapes=[pltpu.VMEM((d, 2 * D_EXPERT), BF16), pltpu.VMEM((D_EXPERT, d), BF16)],
        ),
        out_shape=jax.ShapeDtypeStruct((nc, n_rows, w), xs.dtype),
        compiler_params=_params(("arbitrary",)),
    )(tile_expert, n_tiles, xs, w_gate, w_up, w_down)


def _combine_kernel(x1_ref, route_ref, mod_ref, g_ref, o_ref):
    rec = route_ref[0]
    w0, w1 = rec[:, 4:5], rec[:, 5:6]
    lo0, hi0 = _unpack_rows(g_ref.at[0])
    lo1, hi1 = _unpack_rows(g_ref.at[1])
    moe = jnp.concatenate([w0 * lo0 + w1 * lo1, w0 * hi0 + w1 * hi1], axis=1)
    o_ref[0] = x1_ref[0] + mod_ref[0][5:6] * moe


def _combine(x1, route, mod, g):
    b, s, d = x1.shape
    tm = TOK_TILE
    nj = s // tm
    return pl.pallas_call(
        _combine_kernel,
        grid=(b, nj),
        in_specs=[
            pl.BlockSpec((1, tm, d), lambda i, j: (i, j, 0)),
            pl.BlockSpec((1, tm, LANES), lambda i, j: (i, j, 0)),
            pl.BlockSpec((1, 6, d), lambda i, j: (i, 0, 0)),
            pl.BlockSpec((2, PACK_CHUNKS, tm, LANES), lambda i, j: (0, 0, i * nj + j, 0)),
        ],
        out_specs=pl.BlockSpec((1, tm, d), lambda i, j: (i, j, 0)),
        out_shape=jax.ShapeDtypeStruct((b, s, d), F32),
        compiler_params=_params(("arbitrary", "arbitrary")),
    )(x1, route, mod, g)


def _moe(x1, h2p, route, counts, mod, w_gate, w_up, w_down):
    b, s, d = x1.shape
    t = b * s
    tr = ROW_TILE
    rec = route.reshape(t, LANES)
    eid = rec[:, 0:2].astype(jnp.int32)
    rank = rec[:, 2:4].astype(jnp.int32)
    cnt = counts[0, :N_EXPERTS].astype(jnp.int32)
    padded = (cnt + tr - 1) // tr * tr
    ends = jnp.cumsum(padded)
    pos = (ends - padded)[eid] + rank
    idx0, idx1 = pos[:, 0], pos[:, 1]
    max_tiles = 2 * t // tr + N_EXPERTS
    n_tiles = (ends[-1] // tr).astype(jnp.int32)
    tile_ids = jnp.arange(max_tiles, dtype=jnp.int32)
    last_tile = jnp.minimum(tile_ids, n_tiles - 1)
    tile_expert = jnp.sum((ends // tr)[None, :] <= last_tile[:, None], axis=1)
    tile_expert = jnp.minimum(tile_expert, N_EXPERTS - 1).astype(jnp.int32)
    xs = _sc_scatter_rows(h2p, idx0, idx1, max_tiles * tr)
    y = _experts(xs, tile_expert, n_tiles.reshape(1), w_gate, w_up, w_down)
    return _combine(x1, route, mod, _sc_gather_rows(y, idx0, idx1))


def kernel(x, c, ctx, c_ctx, w_ada, b_ada, norm1_g, norm2_g, a_w_in, a_q_norm_g, a_w_uq, a_kv_norm_g, a_w_ukv,
           a_q_g, a_k_g, b_w_dw, b_b_dw, b_ln_g, b_ln_b, ab_w_out, c_w_in, c_b_in, c_ln_g, c_ln_b, c_w_s, c_b_s,
           c_w_out, moe_w_group, moe_b_group, moe_w_router, moe_b_router, moe_w_gate, moe_w_up, moe_w_down):
    b, s, d = x.shape
    rows = -(-(b + 1) // 8) * 8
    cvec = jnp.zeros((rows, d), F32).at[:b].set(c).at[b].set(c_ctx)
    mod_all = _ada(cvec, w_ada, b_ada)
    mods = [mod_all[l, :b].reshape(b, 6, d) for l in range(w_ada.shape[0])]
    mod_ctx = mod_all[0, b].reshape(1, 6, d)
    row = lambda v: v.reshape(1, -1)

    win_lat, win_ctx, wuq, wuk, wuv, vones, qg, kg = _mla_weights(a_w_in[0], a_w_uq[0], a_w_ukv[0], a_q_g[0], a_k_g[0])
    n1g = row(norm1_g[0])
    kvng = row(a_kv_norm_g[0])
    q, kx, vx, z = _mla_proj(x, mods[0], n1g, win_lat, row(a_q_norm_g[0]), wuq, kvng, wuk, wuv, vones, qg, kg,
                             _rope_tables(s))
    kc, vc = _ctx_proj(ctx, mod_ctx, n1g, win_ctx, kvng, wuk, wuv, vones, kg)
    attn = _attention(q, kx, kc, vx, vc)
    wrt, brt = _router_weights(moe_w_group[0], moe_b_group[0], moe_w_router[0], moe_b_router[0])
    x1, h2, route, counts = _mix0(x, attn, z, mods[0], b_w_dw[0], b_b_dw[0], b_ln_g[0], b_ln_b[0], ab_w_out[0],
                                  row(norm2_g[0]), wrt, brt)
    x = _moe(x1, h2, route, counts, mods[0], moe_w_gate[0], moe_w_up[0], moe_w_down[0])

    wrt, brt = _router_weights(moe_w_group[1], moe_b_group[1], moe_w_router[1], moe_b_router[1])
    x1, h2, route, counts = _gmlp(x, mods[1], row(norm1_g[1]), c_w_in[0], c_b_in[0], c_ln_g[0], c_ln_b[0],
                                  c_w_s[0], c_b_s[0], c_w_out[0], row(norm2_g[1]), wrt, brt)
    return _moe(x1, h2, route, counts, mods[1], moe_w_gate[1], moe_w_up[1], moe_w_down[1])
```

```python
import functools
import math

import jax
import jax.numpy as jnp
from jax import lax
from jax.experimental import pallas as pl
from jax.experimental.pallas import tpu as pltpu
from jax.experimental.pallas import tpu_sc as plsc

F32 = jnp.float32
BF16 = jnp.bfloat16
HIGHEST = lax.Precision.HIGHEST

N_HEADS = 8
Q_LORA = 384
KV_LORA = 256
NOPE_DIM = 64
ROPE_DIM = 32
QK_DIM = NOPE_DIM + ROPE_DIM
V_DIM = 64
ROPE_THETA = 10000.0
GRID_W = 64
CONV_CH = 512
CONV_W = 31
CHUNK = 128
SG_GROUPS = 8
N_GROUPS = 4
EXPERTS_PER_GROUP = 8
N_EXPERTS = N_GROUPS * EXPERTS_PER_GROUP
D_EXPERT = 256

LANES = 128
PACK_CHUNKS = 4
SC_CORES = 2
SC_SUBCORES = 16
SC_ROWS = 128
HEAD_PAD = LANES
HALO = 16

TOK_TILE = 256
PROJ_TILE = 256
GMLP_TILE = 512
Q_TILE = 256
HEADS_PER_STEP = 8
ROW_TILE = 512
VMEM_LIMIT = 56 << 20


def _params(sem, vmem=VMEM_LIMIT):
    return pltpu.CompilerParams(dimension_semantics=sem, vmem_limit_bytes=vmem)


def _const_spec(shape):
    nd = len(shape)
    return pl.BlockSpec(shape, lambda *_: (0,) * nd)


def _rms(x, eps=1e-6):
    return x * lax.rsqrt(jnp.mean(x * x, axis=-1, keepdims=True) + eps)


def _silu(x):
    return x * jax.nn.sigmoid(x)


def _bf16_bits(x):
    u = pltpu.bitcast(x, jnp.uint32)
    return u + jnp.uint32(0x7FFF) + ((u >> 16) & jnp.uint32(1))


def _pack_rows(x, ref):
    n = x.shape[1] // 2
    w = (_bf16_bits(x[:, :n]) >> 16) | (_bf16_bits(x[:, n:]) & jnp.uint32(0xFFFF0000))
    for c in range(PACK_CHUNKS):
        ref[c] = w[:, c * LANES:(c + 1) * LANES]


def _unpack_rows(ref):
    w = jnp.concatenate([ref[c] for c in range(PACK_CHUNKS)], axis=1)
    return pltpu.bitcast(w << 16, F32), pltpu.bitcast(w & jnp.uint32(0xFFFF0000), F32)


def _ada_kernel(c_ref, w_ref, b_ref, o_ref):
    sc = _silu(c_ref[...])
    o_ref[0] = jnp.dot(sc, w_ref[0], preferred_element_type=F32, precision=HIGHEST) + b_ref[0]


def _ada(cvec, w_ada, b_ada):
    depth, d, n = w_ada.shape
    rows = cvec.shape[0]
    tn = 1536
    return pl.pallas_call(
        _ada_kernel,
        grid=(depth, n // tn),
        in_specs=[
            pl.BlockSpec((rows, d), lambda l, j: (0, 0)),
            pl.BlockSpec((1, d, tn), lambda l, j: (l, 0, j)),
            pl.BlockSpec((1, 1, tn), lambda l, j: (l, 0, j)),
        ],
        out_specs=pl.BlockSpec((1, rows, tn), lambda l, j: (l, 0, j)),
        out_shape=jax.ShapeDtypeStruct((depth, rows, n), F32),
        compiler_params=_params(("arbitrary", "arbitrary")),
    )(cvec, w_ada, b_ada.reshape(depth, 1, n))


def _rope(t, rc, rs1, rs2):
    return t * rc + pltpu.roll(t, HEAD_PAD - ROPE_DIM // 2, 1) * rs1 + pltpu.roll(t, ROPE_DIM // 2, 1) * rs2


def _head_norm(t, gain):
    ss = jnp.sum(t * t, axis=-1, keepdims=True) * (1.0 / QK_DIM)
    return t * lax.rsqrt(ss + 1e-6) * gain


def _kv_heads(kvl, kr, kvng, wuk_ref, wuv_ref, vones, kg, rope, k_ref, v_ref):
    kvn = (_rms(kvl) * kvng).astype(BF16)
    kfull = jnp.dot(kvn, wuk_ref[...], preferred_element_type=F32)
    vfull = jnp.dot(kvn, wuv_ref[...], preferred_element_type=F32) + vones
    for h in range(N_HEADS):
        sl = slice(h * HEAD_PAD, (h + 1) * HEAD_PAD)
        kn = _head_norm(kfull[:, sl] + kr, kg)
        if rope is not None:
            kn = _rope(kn, *rope)
        k_ref[0, h] = kn.astype(BF16)
        v_ref[0, h] = vfull[:, sl].astype(BF16)


def _modulated_norm(x, mod, gain, shift_row, scale_row):
    return _rms(x) * gain * (1.0 + mod[scale_row:scale_row + 1]) + mod[shift_row:shift_row + 1]


def _mla_proj_kernel(x_ref, mod_ref, n1g_ref, win_ref, qng_ref, wuq_ref, kvng_ref, wuk_ref, wuv_ref,
                     vones_ref, qg_ref, kg_ref, rc_ref, rs1_ref, rs2_ref,
                     q_ref, k_ref, v_ref, z_ref):
    h = _modulated_norm(x_ref[0], mod_ref[0], n1g_ref[...], 0, 1).astype(BF16)
    proj = jnp.dot(h, win_ref[...], preferred_element_type=F32)
    o_kv = Q_LORA
    o_a = o_kv + KV_LORA
    o_b = o_a + CONV_CH
    o_kr = o_b + CONV_CH
    rope = (rc_ref[...], rs1_ref[...], rs2_ref[...])
    qn = (_rms(proj[:, :o_kv]) * qng_ref[...]).astype(BF16)
    qfull = jnp.dot(qn, wuq_ref[...], preferred_element_type=F32)
    qg = qg_ref[...]
    for hd in range(N_HEADS):
        sl = slice(hd * HEAD_PAD, (hd + 1) * HEAD_PAD)
        q_ref[0, hd] = _rope(_head_norm(qfull[:, sl], qg), *rope).astype(BF16)
    _kv_heads(proj[:, o_kv:o_a], proj[:, o_kr:o_kr + HEAD_PAD], kvng_ref[...], wuk_ref, wuv_ref,
              vones_ref[...], kg_ref[...], rope, k_ref, v_ref)
    z_ref[0] = (proj[:, o_a:o_b] * jax.nn.sigmoid(proj[:, o_b:o_kr])).astype(BF16)


def _ctx_proj_kernel(x_ref, mod_ref, n1g_ref, win_ref, kvng_ref, wuk_ref, wuv_ref, vones_ref, kg_ref,
                     k_ref, v_ref):
    h = _modulated_norm(x_ref[0], mod_ref[0], n1g_ref[...], 0, 1).astype(BF16)
    proj = jnp.dot(h, win_ref[...], preferred_element_type=F32)
    _kv_heads(proj[:, :KV_LORA], proj[:, KV_LORA:KV_LORA + HEAD_PAD], kvng_ref[...], wuk_ref, wuv_ref,
              vones_ref[...], kg_ref[...], None, k_ref, v_ref)


def _rope_tables(s):
    t = jnp.arange(s)
    row = (t // GRID_W).astype(F32)
    col = (t % GRID_W).astype(F32)
    n_freq = ROPE_DIM // 4
    inv_freq = ROPE_THETA ** (-jnp.arange(n_freq, dtype=F32) / n_freq)
    ang = jnp.concatenate([row[:, None] * inv_freq, col[:, None] * inv_freq], axis=-1)
    cos, sin = jnp.cos(ang), jnp.sin(ang)
    half = ROPE_DIM // 2
    zeros = jnp.zeros((s, half), F32)
    tail = jnp.zeros((s, HEAD_PAD - QK_DIM), F32)
    rc = jnp.concatenate([jnp.ones((s, NOPE_DIM), F32), cos, cos, tail], axis=-1)
    rs1 = jnp.concatenate([jnp.zeros((s, NOPE_DIM), F32), -sin, zeros, tail], axis=-1)
    rs2 = jnp.concatenate([jnp.zeros((s, NOPE_DIM), F32), zeros, sin, tail], axis=-1)
    return rc, rs1, rs2


def _pad_heads(w, left=0):
    k, nh, width = w.shape
    return jnp.pad(w, ((0, 0), (0, 0), (left, HEAD_PAD - width - left))).reshape(k, nh * HEAD_PAD)


def _ones_lane(h):
    return V_DIM if h % 2 == 0 else 0


def _mla_weights(w_in, w_uq, w_ukv, q_g, k_g):
    d = w_in.shape[0]
    o1, o2, o3 = Q_LORA, Q_LORA + KV_LORA, Q_LORA + KV_LORA + ROPE_DIM
    kr_block = jnp.pad(w_in[:, o2:o3], ((0, 0), (NOPE_DIM, HEAD_PAD - QK_DIM)))
    glu = w_in[:, o3:]
    win_lat = jnp.concatenate([w_in[:, :o1], w_in[:, o1:o2], glu, kr_block], axis=1).astype(BF16)
    win_ctx = jnp.concatenate([w_in[:, o1:o2], kr_block], axis=1).astype(BF16)
    wuq = _pad_heads(w_uq.reshape(Q_LORA, N_HEADS, QK_DIM)).astype(BF16)
    ukv = w_ukv.reshape(KV_LORA, N_HEADS, NOPE_DIM + V_DIM)
    wuk = _pad_heads(ukv[:, :, :NOPE_DIM]).astype(BF16)
    uv = ukv[:, :, NOPE_DIM:].reshape(KV_LORA, N_HEADS // 2, 2, V_DIM)
    wuv = jnp.stack([jnp.pad(uv[:, :, 0], ((0, 0), (0, 0), (0, HEAD_PAD - V_DIM))),
                     jnp.pad(uv[:, :, 1], ((0, 0), (0, 0), (HEAD_PAD - V_DIM, 0)))], axis=2)
    wuv = wuv.reshape(KV_LORA, N_HEADS * HEAD_PAD).astype(BF16)
    lane = jnp.arange(N_HEADS * HEAD_PAD)
    ones_at = jnp.where((lane // HEAD_PAD) % 2 == 0, V_DIM, 0)
    vones = (lane % HEAD_PAD == ones_at).astype(F32).reshape(1, N_HEADS * HEAD_PAD)
    pad = jnp.zeros((HEAD_PAD - QK_DIM,), F32)
    qg = (jnp.concatenate([q_g, pad]) * (QK_DIM ** -0.5 * math.log2(math.e))).reshape(1, HEAD_PAD)
    kg = jnp.concatenate([k_g, pad]).reshape(1, HEAD_PAD)
    return win_lat, win_ctx, wuq, wuk, wuv, vones, qg, kg


def _mla_proj(x, mod, n1g, win, qng, wuq, kvng, wuk, wuv, vones, qg, kg, rope):
    b, s, d = x.shape
    tm = PROJ_TILE
    hp = N_HEADS * HEAD_PAD
    head_out = jax.ShapeDtypeStruct((b, N_HEADS, s, HEAD_PAD), BF16)
    head_spec = pl.BlockSpec((1, N_HEADS, tm, HEAD_PAD), lambda i, j: (i, 0, j, 0))
    rope_spec = pl.BlockSpec((tm, HEAD_PAD), lambda i, j: (j, 0))
    return pl.pallas_call(
        _mla_proj_kernel,
        grid=(b, s // tm),
        in_specs=[
            pl.BlockSpec((1, tm, d), lambda i, j: (i, j, 0)),
            pl.BlockSpec((1, 6, d), lambda i, j: (i, 0, 0)),
            _const_spec((1, d)), _const_spec(win.shape), _const_spec((1, Q_LORA)), _const_spec(wuq.shape),
            _const_spec((1, KV_LORA)), _const_spec(wuk.shape), _const_spec(wuv.shape), _const_spec((1, hp)),
            _const_spec((1, HEAD_PAD)), _const_spec((1, HEAD_PAD)), rope_spec, rope_spec, rope_spec,
        ],
        out_specs=[head_spec, head_spec, head_spec, pl.BlockSpec((1, tm, CONV_CH), lambda i, j: (i, j, 0))],
        out_shape=[head_out, head_out, head_out, jax.ShapeDtypeStruct((b, s, CONV_CH), BF16)],
        compiler_params=_params(("arbitrary", "arbitrary")),
    )(x, mod, n1g, win, qng, wuq, kvng, wuk, wuv, vones, qg, kg, *rope)


def _ctx_proj(ctx, mod_c, n1g, win, kvng, wuk, wuv, vones, kg):
    b, s, d = ctx.shape
    tm = min(TOK_TILE, s)
    hp = N_HEADS * HEAD_PAD
    head_out = jax.ShapeDtypeStruct((b, N_HEADS, s, HEAD_PAD), BF16)
    head_spec = pl.BlockSpec((1, N_HEADS, tm, HEAD_PAD), lambda i, j: (i, 0, j, 0))
    return pl.pallas_call(
        _ctx_proj_kernel,
        grid=(b, s // tm),
        in_specs=[
            pl.BlockSpec((1, tm, d), lambda i, j: (i, j, 0)),
            _const_spec((1, 6, d)),
            _const_spec((1, d)), _const_spec(win.shape), _const_spec((1, KV_LORA)), _const_spec(wuk.shape),
            _const_spec(wuv.shape), _const_spec((1, hp)), _const_spec((1, HEAD_PAD)),
        ],
        out_specs=[head_spec, head_spec],
        out_shape=[head_out, head_out],
        compiler_params=_params(("arbitrary", "arbitrary")),
    )(ctx, mod_c, n1g, win, kvng, wuk, wuv, vones, kg)


def _attn_kernel(q_ref, kx_ref, kc_ref, vx_ref, vc_ref, o_ref):
    nt = (((1,), (1,)), ((), ()))
    outs = []
    for j in range(HEADS_PER_STEP):
        q = q_ref[0, j]
        sx = lax.dot_general(q, kx_ref[0, j], nt, preferred_element_type=F32)
        sc = lax.dot_general(q, kc_ref[0, j], nt, preferred_element_type=F32)
        m = jnp.maximum(jnp.max(sx, axis=-1, keepdims=True), jnp.max(sc, axis=-1, keepdims=True))
        px = jnp.exp2(sx - m).astype(BF16)
        pc = jnp.exp2(sc - m).astype(BF16)
        o = (jnp.dot(px, vx_ref[0, j], preferred_element_type=F32)
             + jnp.dot(pc, vc_ref[0, j], preferred_element_type=F32))
        ol = _ones_lane(j)
        outs.append(o / o[:, ol:ol + 1])
    lane = lax.broadcasted_iota(jnp.int32, outs[0].shape, 1)
    for p in range(HEADS_PER_STEP // 2):
        pair = jnp.where(lane < V_DIM, outs[2 * p], outs[2 * p + 1]).astype(BF16)
        o_ref[0, :, p * HEAD_PAD:(p + 1) * HEAD_PAD] = pair


def _attention(q, kx, kc, vx, vc):
    b, nh, s, hp = q.shape
    c = kc.shape[2]
    tq = Q_TILE
    hs = HEADS_PER_STEP
    kv_spec = lambda n: pl.BlockSpec((1, hs, n, hp), lambda i, h, j: (i, h, 0, 0))
    return pl.pallas_call(
        _attn_kernel,
        grid=(b, nh // hs, s // tq),
        in_specs=[pl.BlockSpec((1, hs, tq, hp), lambda i, h, j: (i, h, j, 0)),
                  kv_spec(s), kv_spec(c), kv_spec(s), kv_spec(c)],
        out_specs=pl.BlockSpec((1, tq, hs * V_DIM), lambda i, h, j: (i, j, h)),
        out_shape=jax.ShapeDtypeStruct((b, s, nh * V_DIM), BF16),
        compiler_params=_params(("arbitrary", "arbitrary", "arbitrary")),
    )(q, kx, kc, vx, vc)


def _route(logits, run):
    tm = logits.shape[0]
    lane = lax.broadcasted_iota(jnp.int32, logits.shape, 1)
    big = jnp.int32(1 << 20)
    neg = jnp.float32(-jnp.inf)
    gmask = (lane >= N_EXPERTS) & (lane < N_EXPERTS + N_GROUPS)
    gl = jnp.where(gmask, logits, neg)
    gm = jnp.max(gl, axis=-1, keepdims=True)
    gidx = jnp.min(jnp.where(gl == gm, lane, big), axis=-1, keepdims=True) - N_EXPERTS
    gw = 1.0 / jnp.sum(jnp.where(gmask, jnp.exp(gl - gm), 0.0), axis=-1, keepdims=True)
    lo = gidx * EXPERTS_PER_GROUP
    el = jnp.where((lane >= lo) & (lane < lo + EXPERTS_PER_GROUP), logits, neg)
    m1 = jnp.max(el, axis=-1, keepdims=True)
    i1 = jnp.min(jnp.where(el == m1, lane, big), axis=-1, keepdims=True)
    el2 = jnp.where(lane == i1, neg, el)
    m2 = jnp.max(el2, axis=-1, keepdims=True)
    i2 = jnp.min(jnp.where(el2 == m2, lane, big), axis=-1, keepdims=True)
    t = jnp.exp(m2 - m1)
    w1 = gw / (1.0 + t)
    w2 = w1 * t
    hot1 = lane == i1
    hot2 = lane == i2
    onehot = jnp.where(hot1, 1.0, jnp.where(hot2, 1.0, 0.0))
    r = lax.broadcasted_iota(jnp.int32, (tm, tm), 0)
    c = lax.broadcasted_iota(jnp.int32, (tm, tm), 1)
    lower = jnp.where(c < r, 1.0, 0.0).astype(BF16)
    before = jnp.dot(lower, onehot.astype(BF16), preferred_element_type=F32) + run
    rank1 = jnp.sum(jnp.where(hot1, before, 0.0), axis=-1, keepdims=True)
    rank2 = jnp.sum(jnp.where(hot2, before, 0.0), axis=-1, keepdims=True)
    rec = jnp.where(lane == 0, i1.astype(F32),
          jnp.where(lane == 1, i2.astype(F32),
          jnp.where(lane == 2, rank1,
          jnp.where(lane == 3, rank2,
          jnp.where(lane == 4, w1,
          jnp.where(lane == 5, w2, 0.0))))))
    return rec, run + jnp.sum(onehot, axis=0, keepdims=True)


def _post_mixer(x, y, mod, n2g, wrt_ref, brt_ref, run_ref, x1_ref, h2_ref, route_ref, cnt_ref):
    @pl.when((pl.program_id(0) == 0) & (pl.program_id(1) == 0))
    def _():
        run_ref[...] = jnp.zeros_like(run_ref)

    x1 = x + mod[2:3] * y
    h2 = _rms(x1) * n2g * (1.0 + mod[4:5]) + mod[3:4]
    h_hi = h2.astype(BF16)
    h_lo = (h2 - h_hi.astype(F32)).astype(BF16)
    hi_prod = jnp.dot(h_hi, wrt_ref[...], preferred_element_type=F32)
    lo_prod = jnp.dot(h_lo, wrt_ref[:, :LANES], preferred_element_type=F32)
    logits = hi_prod[:, :LANES] + hi_prod[:, LANES:] + lo_prod + brt_ref[...]
    rec, run = _route(logits, run_ref[...])
    run_ref[...] = run
    cnt_ref[...] = run
    x1_ref[0] = x1
    _pack_rows(h2, h2_ref)
    route_ref[0] = rec


def _router_weights(w_group, b_group, w_router, b_router):
    d = w_group.shape[0]
    wr = jnp.transpose(w_router, (1, 0, 2)).reshape(d, N_EXPERTS)
    wrt = jnp.zeros((d, LANES), F32).at[:, :N_EXPERTS].set(wr).at[:, N_EXPERTS:N_EXPERTS + N_GROUPS].set(w_group)
    brt = jnp.zeros((1, LANES), F32).at[0, :N_EXPERTS].set(b_router.reshape(-1))
    brt = brt.at[0, N_EXPERTS:N_EXPERTS + N_GROUPS].set(b_group)
    w_hi = wrt.astype(BF16)
    w_lo = (wrt - w_hi.astype(F32)).astype(BF16)
    return jnp.concatenate([w_hi, w_lo], axis=1), brt


def _mixer_outs(b, s, d, tm):
    tok = lambda n: pl.BlockSpec((1, tm, n), lambda i, j: (i, j, 0))
    nj = s // tm
    packed = pl.BlockSpec((PACK_CHUNKS, tm, LANES), lambda i, j: (0, i * nj + j, 0))
    specs = [tok(d), packed, tok(LANES), _const_spec((1, LANES))]
    shapes = [jax.ShapeDtypeStruct((b, s, d), F32), jax.ShapeDtypeStruct((PACK_CHUNKS, b * s, LANES), jnp.uint32),
              jax.ShapeDtypeStruct((b, s, LANES), F32), jax.ShapeDtypeStruct((1, LANES), F32)]
    return specs, shapes


def _mix0_kernel(x_ref, attn_ref, z_ref, zp_ref, zn_ref, mod_ref, wdw_ref, bdw_ref, lng_ref, lnb_ref,
                 woa_ref, woc_ref, n2g_ref, wrt_ref, brt_ref,
                 x1_ref, h2_ref, route_ref, cnt_ref, run_ref, zext_ref):
    j = pl.program_id(1)
    tm = z_ref.shape[1]
    zext_ref[0:HALO] = jnp.where(j > 0, zp_ref[0].astype(F32), 0.0)
    zext_ref[HALO:HALO + tm] = z_ref[0].astype(F32)
    zext_ref[HALO + tm:] = jnp.where(j < pl.num_programs(1) - 1, zn_ref[0].astype(F32), 0.0)
    wdw = wdw_ref[...]
    acc = jnp.zeros((tm, CONV_CH), F32) + bdw_ref[...]
    base = HALO - CONV_W // 2
    sub = 8
    for r in range(sub):
        part = None
        for o in range(r, base + CONV_W, sub):
            if o < base:
                continue
            term = zext_ref[o - r:o - r + tm + sub] * wdw[o - base:o - base + 1]
            part = term if part is None else part + term
        acc = acc + part[r:r + tm]
    mu = jnp.mean(acc, axis=-1, keepdims=True)
    cen = acc - mu
    var = jnp.mean(cen * cen, axis=-1, keepdims=True)
    conv = _silu(cen * lax.rsqrt(var + 1e-5) * lng_ref[...] + lnb_ref[...]).astype(BF16)
    y = (jnp.dot(attn_ref[0], woa_ref[...], preferred_element_type=F32)
         + jnp.dot(conv, woc_ref[...], preferred_element_type=F32))
    _post_mixer(x_ref[0], y, mod_ref[0], n2g_ref[...], wrt_ref, brt_ref, run_ref,
                x1_ref, h2_ref, route_ref, cnt_ref)


def _mix0(x, attn, z, mod, w_dw, b_dw, ln_g, ln_b, w_out, n2g, wrt, brt):
    b, s, d = x.shape
    tm = TOK_TILE
    nh = tm // HALO
    last = s // HALO - 1
    tok = lambda n: pl.BlockSpec((1, tm, n), lambda i, j: (i, j, 0))
    out_specs, out_shapes = _mixer_outs(b, s, d, tm)
    nattn = N_HEADS * V_DIM
    return pl.pallas_call(
        _mix0_kernel,
        grid=(b, s // tm),
        in_specs=[
            tok(d), tok(nattn), tok(CONV_CH),
            pl.BlockSpec((1, HALO, CONV_CH), lambda i, j: (i, jnp.maximum(j * nh - 1, 0), 0)),
            pl.BlockSpec((1, HALO, CONV_CH), lambda i, j: (i, jnp.minimum((j + 1) * nh, last), 0)),
            pl.BlockSpec((1, 6, d), lambda i, j: (i, 0, 0)),
            _const_spec((CONV_W, CONV_CH)), _const_spec((1, CONV_CH)), _const_spec((1, CONV_CH)),
            _const_spec((1, CONV_CH)), _const_spec((nattn, d)), _const_spec((CONV_CH, d)),
            _const_spec((1, d)), _const_spec((d, 2 * LANES)), _const_spec((1, LANES)),
        ],
        out_specs=out_specs,
        out_shape=out_shapes,
        scratch_shapes=[pltpu.VMEM((1, LANES), F32), pltpu.VMEM((tm + 2 * HALO, CONV_CH), F32)],
        compiler_params=_params(("arbitrary", "arbitrary")),
    )(x, attn, z, z, z, mod, w_dw, b_dw.reshape(1, -1), ln_g.reshape(1, -1), ln_b.reshape(1, -1),
      w_out[:nattn].astype(BF16), w_out[nattn:].astype(BF16), n2g, wrt, brt)


def _gmlp_kernel(xp_ref, routep_ref, gp_ref, modp_ref, mod_ref, n1g_ref, win_ref, bin_ref, lng_ref, lnb_ref,
                 ws_ref, bs_ref, wout_ref, n2g_ref, wrt_ref, brt_ref,
                 x1_ref, h2_ref, route_ref, cnt_ref, run_ref, us_ref):
    x = _moe_residual(xp_ref[0], routep_ref[0], modp_ref[0], gp_ref)
    mod = mod_ref[0]
    tm = x.shape[0]
    sg = lng_ref.shape[1]
    gw = sg // SG_GROUPS
    h = _modulated_norm(x, mod, n1g_ref[...], 0, 1).astype(BF16)
    u = jax.nn.gelu(jnp.dot(h, win_ref[:, :sg], preferred_element_type=F32) + bin_ref[:, :sg])
    v = jax.nn.gelu(jnp.dot(h, win_ref[:, sg:], preferred_element_type=F32) + bin_ref[:, sg:])
    mu = jnp.mean(v, axis=-1, keepdims=True)
    cen = v - mu
    var = jnp.mean(cen * cen, axis=-1, keepdims=True)
    vn = (cen * lax.rsqrt(var + 1e-5) * lng_ref[...] + lnb_ref[...]).astype(BF16)
    for c in range(tm // CHUNK):
        rows = slice(c * CHUNK, (c + 1) * CHUNK)
        for g in range(SG_GROUPS):
            cols = slice(g * gw, (g + 1) * gw)
            s = jnp.dot(ws_ref[g], vn[rows, cols], preferred_element_type=F32) + bs_ref[:, cols]
            us_ref[rows, cols] = (u[rows, cols] * s).astype(BF16)
    y = jnp.dot(us_ref[...], wout_ref[...], preferred_element_type=F32)
    _post_mixer(x, y, mod, n2g_ref[...], wrt_ref, brt_ref, run_ref, x1_ref, h2_ref, route_ref, cnt_ref)


def _gmlp(x, route_prev, g_prev, mod_prev, mod, n1g, w_in, b_in, ln_g, ln_b, w_s, b_s, w_out, n2g, wrt, brt):
    b, s, d = x.shape
    tm = GMLP_TILE
    nj = s // tm
    sg = w_out.shape[0]
    gw = sg // SG_GROUPS
    bs_full = jnp.repeat(b_s.T, gw, axis=1)
    out_specs, out_shapes = _mixer_outs(b, s, d, tm)
    return pl.pallas_call(
        _gmlp_kernel,
        grid=(b, nj),
        in_specs=[
            pl.BlockSpec((1, tm, d), lambda i, j: (i, j, 0)),
            pl.BlockSpec((1, tm, LANES), lambda i, j: (i, j, 0)),
            pl.BlockSpec((2, PACK_CHUNKS, tm, LANES), lambda i, j: (0, 0, i * nj + j, 0)),
            pl.BlockSpec((1, 6, d), lambda i, j: (i, 0, 0)),
            pl.BlockSpec((1, 6, d), lambda i, j: (i, 0, 0)),
            _const_spec((1, d)), _const_spec((d, 2 * sg)), _const_spec((1, 2 * sg)),
            _const_spec((1, sg)), _const_spec((1, sg)), _const_spec((SG_GROUPS, CHUNK, CHUNK)),
            _const_spec((CHUNK, sg)), _const_spec((sg, d)),
            _const_spec((1, d)), _const_spec((d, 2 * LANES)), _const_spec((1, LANES)),
        ],
        out_specs=out_specs,
        out_shape=out_shapes,
        scratch_shapes=[pltpu.VMEM((1, LANES), F32), pltpu.VMEM((tm, sg), BF16)],
        compiler_params=_params(("arbitrary", "arbitrary")),
    )(x, route_prev, g_prev, mod_prev, mod, n1g, w_in.astype(BF16), b_in.reshape(1, -1), ln_g.reshape(1, -1),
      ln_b.reshape(1, -1),
      w_s.astype(BF16), bs_full, w_out.astype(BF16), n2g, wrt, brt)


def _sc_mesh():
    return plsc.VectorSubcoreMesh(core_axis_name="c", subcore_axis_name="s",
                                  num_cores=SC_CORES, num_subcores=SC_SUBCORES)


def _sc_worker_base(rows_per_worker):
    return (lax.axis_index("s") * SC_CORES + lax.axis_index("c")) * rows_per_worker


def _sc_scatter_rows(rows, idx0, idx1, n_out):
    nc, t, w = rows.shape
    per = t // (SC_CORES * SC_SUBCORES)
    cb = SC_ROWS

    def body(rows_hbm, i0_hbm, i1_hbm, out_hbm, i0_v, i1_v, rows_v, sem):
        base0 = _sc_worker_base(per)

        @pl.loop(0, per // cb)
        def _(j):
            base = pl.multiple_of(base0 + j * cb, cb)
            pltpu.sync_copy(i0_hbm.at[pl.ds(base, cb)], i0_v)
            pltpu.sync_copy(i1_hbm.at[pl.ds(base, cb)], i1_v)
            for c in range(nc):
                pltpu.sync_copy(rows_hbm.at[c, pl.ds(base, cb)], rows_v.at[c])
            copies = []
            for c in range(nc):
                copies.append(pltpu.async_copy(rows_v.at[c], out_hbm.at[c].at[i0_v], sem))
                copies.append(pltpu.async_copy(rows_v.at[c], out_hbm.at[c].at[i1_v], sem))
            for cp in copies:
                cp.wait()

    return pl.kernel(
        body,
        out_type=jax.ShapeDtypeStruct((nc, n_out, w), rows.dtype),
        mesh=_sc_mesh(),
        scratch_types=[pltpu.VMEM((cb,), jnp.int32), pltpu.VMEM((cb,), jnp.int32),
                       pltpu.VMEM((nc, cb, w), rows.dtype), pltpu.SemaphoreType.DMA],
    )(rows, idx0, idx1)


def _sc_gather_rows(table, idx0, idx1):
    nc, _, w = table.shape
    t = idx0.shape[0]
    per = t // (SC_CORES * SC_SUBCORES)
    cb = SC_ROWS

    def body(table_hbm, i0_hbm, i1_hbm, out_hbm, idx_v, rows_v, sem):
        base0 = _sc_worker_base(per)

        @pl.loop(0, per // cb)
        def _(j):
            base = pl.multiple_of(base0 + j * cb, cb)
            for k, idx_hbm in enumerate((i0_hbm, i1_hbm)):
                pltpu.sync_copy(idx_hbm.at[pl.ds(base, cb)], idx_v)
                copies = [pltpu.async_copy(table_hbm.at[c].at[idx_v], rows_v.at[c], sem) for c in range(nc)]
                for cp in copies:
                    cp.wait()
                for c in range(nc):
                    pltpu.sync_copy(rows_v.at[c], out_hbm.at[k, c, pl.ds(base, cb)])

    return pl.kernel(
        body,
        out_type=jax.ShapeDtypeStruct((2, nc, t, w), table.dtype),
        mesh=_sc_mesh(),
        scratch_types=[pltpu.VMEM((cb,), jnp.int32), pltpu.VMEM((nc, cb, w), table.dtype),
                       pltpu.SemaphoreType.DMA],
    )(table, idx0, idx1)


def _expert_kernel(te_ref, nt_ref, xs_ref, wg_ref, wu_ref, wd_ref, y_ref, wgu_s, wd_s):
    i = pl.program_id(0)
    half = wgu_s.shape[0] // 2

    @pl.when(i < nt_ref[0])
    def _():
        prev = te_ref[jnp.maximum(i - 1, 0)]

        @pl.when((i == 0) | (te_ref[i] != prev))
        def _():
            wgu_s[:, :D_EXPERT] = wg_ref[0, 0].astype(BF16)
            wgu_s[:, D_EXPERT:] = wu_ref[0, 0].astype(BF16)
            wd_s[...] = wd_ref[0, 0].astype(BF16)

        lo, hi = _unpack_rows(xs_ref)
        gu = (jnp.dot(lo.astype(BF16), wgu_s[:half], preferred_element_type=F32)
              + jnp.dot(hi.astype(BF16), wgu_s[half:], preferred_element_type=F32))
        act = (_silu(gu[:, :D_EXPERT]) * gu[:, D_EXPERT:]).astype(BF16)
        _pack_rows(jnp.dot(act, wd_s[...], preferred_element_type=F32), y_ref)

    @pl.when(i >= nt_ref[0])
    def _():
        y_ref[...] = jnp.zeros_like(y_ref)


def _experts(xs, tile_expert, n_tiles, w_gate, w_up, w_down, layer):
    nc, n_rows, w = xs.shape
    d = w_gate.shape[2]
    tr = ROW_TILE
    row_map = lambda i, te, nt: (0, jnp.minimum(i, nt[0] - 1), 0)
    return pl.pallas_call(
        _expert_kernel,
        grid_spec=pltpu.PrefetchScalarGridSpec(
            num_scalar_prefetch=2,
            grid=(n_rows // tr,),
            in_specs=[
                pl.BlockSpec((nc, tr, w), row_map),
                pl.BlockSpec((1, 1, d, D_EXPERT), lambda i, te, nt: (layer, te[i], 0, 0)),
                pl.BlockSpec((1, 1, d, D_EXPERT), lambda i, te, nt: (layer, te[i], 0, 0)),
                pl.BlockSpec((1, 1, D_EXPERT, d), lambda i, te, nt: (layer, te[i], 0, 0)),
            ],
            out_specs=pl.BlockSpec((nc, tr, w), lambda i, te, nt: (0, i, 0)),
            scratch_shapes=[pltpu.VMEM((d, 2 * D_EXPERT), BF16), pltpu.VMEM((D_EXPERT, d), BF16)],
        ),
        out_shape=jax.ShapeDtypeStruct((nc, n_rows, w), xs.dtype),
        compiler_params=_params(("arbitrary",)),
    )(tile_expert, n_tiles, xs, w_gate, w_up, w_down)


def _moe_residual(x1, rec, mod, g_ref):
    w0, w1 = rec[:, 4:5], rec[:, 5:6]
    lo0, hi0 = _unpack_rows(g_ref.at[0])
    lo1, hi1 = _unpack_rows(g_ref.at[1])
    moe = jnp.concatenate([w0 * lo0 + w1 * lo1, w0 * hi0 + w1 * hi1], axis=1)
    return x1 + mod[5:6] * moe


def _combine_kernel(x1_ref, route_ref, mod_ref, g_ref, o_ref):
    o_ref[0] = _moe_residual(x1_ref[0], route_ref[0], mod_ref[0], g_ref)


def _combine(x1, route, mod, g):
    b, s, d = x1.shape
    tm = TOK_TILE
    nj = s // tm
    return pl.pallas_call(
        _combine_kernel,
        grid=(b, nj),
        in_specs=[
            pl.BlockSpec((1, tm, d), lambda i, j: (i, j, 0)),
            pl.BlockSpec((1, tm, LANES), lambda i, j: (i, j, 0)),
            pl.BlockSpec((1, 6, d), lambda i, j: (i, 0, 0)),
            pl.BlockSpec((2, PACK_CHUNKS, tm, LANES), lambda i, j: (0, 0, i * nj + j, 0)),
        ],
        out_specs=pl.BlockSpec((1, tm, d), lambda i, j: (i, j, 0)),
        out_shape=jax.ShapeDtypeStruct((b, s, d), F32),
        compiler_params=_params(("arbitrary", "arbitrary")),
    )(x1, route, mod, g)


def _moe_rows(h2p, route, counts, w_gate, w_up, w_down, layer):
    t = h2p.shape[1]
    tr = ROW_TILE
    rec = route.reshape(t, LANES)
    eid = rec[:, 0:2].astype(jnp.int32)
    rank = rec[:, 2:4].astype(jnp.int32)
    cnt = counts[0, :N_EXPERTS].astype(jnp.int32)
    padded = (cnt + tr - 1) // tr * tr
    ends = jnp.cumsum(padded)
    pos = (ends - padded)[eid] + rank
    idx0, idx1 = pos[:, 0], pos[:, 1]
    max_tiles = 2 * t // tr + N_EXPERTS
    n_tiles = (ends[-1] // tr).astype(jnp.int32)
    tile_ids = jnp.arange(max_tiles, dtype=jnp.int32)
    last_tile = jnp.minimum(tile_ids, n_tiles - 1)
    tile_expert = jnp.sum((ends // tr)[None, :] <= last_tile[:, None], axis=1)
    tile_expert = jnp.minimum(tile_expert, N_EXPERTS - 1).astype(jnp.int32)
    xs = _sc_scatter_rows(h2p, idx0, idx1, max_tiles * tr)
    y = _experts(xs, tile_expert, n_tiles.reshape(1), w_gate, w_up, w_down, layer)
    return _sc_gather_rows(y, idx0, idx1)


def kernel(x, c, ctx, c_ctx, w_ada, b_ada, norm1_g, norm2_g, a_w_in, a_q_norm_g, a_w_uq, a_kv_norm_g, a_w_ukv,
           a_q_g, a_k_g, b_w_dw, b_b_dw, b_ln_g, b_ln_b, ab_w_out, c_w_in, c_b_in, c_ln_g, c_ln_b, c_w_s, c_b_s,
           c_w_out, moe_w_group, moe_b_group, moe_w_router, moe_b_router, moe_w_gate, moe_w_up, moe_w_down):
    b, s, d = x.shape
    rows = -(-(b + 1) // 8) * 8
    cvec = jnp.zeros((rows, d), F32).at[:b].set(c).at[b].set(c_ctx)
    mod_all = _ada(cvec, w_ada, b_ada)
    mods = [mod_all[l, :b].reshape(b, 6, d) for l in range(w_ada.shape[0])]
    mod_ctx = mod_all[0, b].reshape(1, 6, d)
    row = lambda v: v.reshape(1, -1)

    win_lat, win_ctx, wuq, wuk, wuv, vones, qg, kg = _mla_weights(a_w_in[0], a_w_uq[0], a_w_ukv[0], a_q_g[0], a_k_g[0])
    n1g = row(norm1_g[0])
    kvng = row(a_kv_norm_g[0])
    q, kx, vx, z = _mla_proj(x, mods[0], n1g, win_lat, row(a_q_norm_g[0]), wuq, kvng, wuk, wuv, vones, qg, kg,
                             _rope_tables(s))
    kc, vc = _ctx_proj(ctx, mod_ctx, n1g, win_ctx, kvng, wuk, wuv, vones, kg)
    attn = _attention(q, kx, kc, vx, vc)
    wrt, brt = _router_weights(moe_w_group[0], moe_b_group[0], moe_w_router[0], moe_b_router[0])
    x1, h2, route, counts = _mix0(x, attn, z, mods[0], b_w_dw[0], b_b_dw[0], b_ln_g[0], b_ln_b[0], ab_w_out[0],
                                  row(norm2_g[0]), wrt, brt)
    g = _moe_rows(h2, route, counts, moe_w_gate, moe_w_up, moe_w_down, 0)

    wrt, brt = _router_weights(moe_w_group[1], moe_b_group[1], moe_w_router[1], moe_b_router[1])
    x1, h2, route, counts = _gmlp(x1, route, g, mods[0], mods[1], row(norm1_g[1]), c_w_in[0], c_b_in[0], c_ln_g[0],
                                  c_ln_b[0], c_w_s[0], c_b_s[0], c_w_out[0], row(norm2_g[1]), wrt, brt)
    g = _moe_rows(h2, route, counts, moe_w_gate, moe_w_up, moe_w_down, 1)
    return _combine(x1, route, mods[1], g)
```
